```python
import math
import jax, jax.numpy as jnp
from jax import lax
import numpy as np

D_MODEL = 1024
BATCH = 16
SEQ = 4096
DEPTH = 1

HEAD_DIM = 64
N_HEADS_DIL = 8
DIL_PATTERNS = ((128, 1), (512, 4), (2048, 16))
ROPE_DIM_DIL = HEAD_DIM // 4
ROPE_THETA = 500000.0
ATTN_BLOCK = 128
N_HEADS_MLA = 8
Q_LORA = 384
KV_LORA = 128
QK_NOPE = 64
QK_ROPE = 32
V_HEAD = 64
DIL_WIDTH = N_HEADS_DIL * HEAD_DIM
MLA_WIDTH = N_HEADS_MLA * V_HEAD
MIX_WIDTH = DIL_WIDTH + MLA_WIDTH
IN_SPLITS = (DIL_WIDTH, 2 * DIL_WIDTH, 3 * DIL_WIDTH,
             3 * DIL_WIDTH + Q_LORA, 3 * DIL_WIDTH + Q_LORA + KV_LORA)
IN_WIDTH = 3 * DIL_WIDTH + Q_LORA + KV_LORA + QK_ROPE
N_EXPERTS = 256
TOP_K = 8
N_GROUP = 8
TOPK_GROUP = 4
EXPERT_FF = 256
SHARED_FF = 256
ROUTED_SCALE = 2.5
EXPERT_BLOCK = 128
DEEPNORM_ALPHA = (2.0 * DEPTH) ** 0.25
DEEPNORM_BETA = (8.0 * DEPTH) ** -0.25
LN_EPS = 1e-5
RMS_EPS = 1e-6
NEG = -1e30

kernel_name = "hymba_dilated_mla_moe_deepnorm"


def _rmsnorm(x, g):
    xf = x.astype(jnp.float32)
    y = xf * lax.rsqrt(jnp.mean(xf * xf, axis=-1, keepdims=True) + RMS_EPS)
    return (y * g.astype(jnp.float32)).astype(x.dtype)


def _layernorm(x, g, b):
    xf = x.astype(jnp.float32)
    mu = jnp.mean(xf, axis=-1, keepdims=True)
    var = jnp.mean(jnp.square(xf - mu), axis=-1, keepdims=True)
    y = (xf - mu) * lax.rsqrt(var + LN_EPS) * g.astype(jnp.float32) + b.astype(jnp.float32)
    return y.astype(x.dtype)


def _rope_table(seq, dim):
    inv = ROPE_THETA ** (-jnp.arange(0, dim, 2, dtype=jnp.float32) / dim)
    ang = jnp.arange(seq, dtype=jnp.float32)[:, None] * inv[None, :]
    return jnp.cos(ang), jnp.sin(ang)


def _apply_rope(x, cos, sin):
    half = x.shape[-1] // 2
    xf = x.astype(jnp.float32)
    x1, x2 = xf[..., :half], xf[..., half:]
    c, s = cos[None, :, None, :], sin[None, :, None, :]
    return jnp.concatenate([x1 * c - x2 * s, x2 * c + x1 * s], axis=-1).astype(x.dtype)


def _partial_rope(x, cos, sin):
    return jnp.concatenate([_apply_rope(x[..., :ROPE_DIM_DIL], cos, sin),
                            x[..., ROPE_DIM_DIL:]], axis=-1)


def _dilated_window_attention(q, k, v, window, dilation):
    B, S, H, Dh = q.shape
    d = dilation
    W = window // d
    L = S // d
    nb = -(-L // ATTN_BLOCK)
    Lp = nb * ATTN_BLOCK
    qs = q.reshape(B, L, d, H, Dh)
    ks = k.reshape(B, L, d, H, Dh)
    vs = v.reshape(B, L, d, H, Dh)
    qs = jnp.pad(qs, ((0, 0), (0, Lp - L), (0, 0), (0, 0), (0, 0)))
    ks = jnp.pad(ks, ((0, 0), (W, Lp - L), (0, 0), (0, 0), (0, 0)))
    vs = jnp.pad(vs, ((0, 0), (W, Lp - L), (0, 0), (0, 0), (0, 0)))
    qb = qs.reshape(B, nb, ATTN_BLOCK, d, H, Dh)
    kidx = jnp.arange(nb)[:, None] * ATTN_BLOCK + jnp.arange(ATTN_BLOCK + W)[None, :]
    kb = ks[:, kidx]
    vb = vs[:, kidx]
    s = jnp.einsum('bnqrhd,bnkrhd->bnrhqk', qb, kb,
                   preferred_element_type=jnp.float32) * (Dh ** -0.5)
    rel = jnp.arange(ATTN_BLOCK)[:, None] + W - jnp.arange(ATTN_BLOCK + W)[None, :]
    key_j = kidx - W
    valid = ((rel >= 0) & (rel <= W))[None, :, :] & (key_j >= 0)[:, None, :]
    s = jnp.where(valid[None, :, None, None], s, NEG)
    m = jnp.max(s, axis=-1)
    p = jnp.exp(s - m[..., None])
    l = jnp.sum(p, axis=-1)
    o = jnp.einsum('bnrhqk,bnkrhd->bnqrhd', p.astype(v.dtype), vb,
                   preferred_element_type=jnp.float32)
    o = o / jnp.transpose(l, (0, 1, 4, 2, 3))[..., None]
    o = o.reshape(B, Lp, d, H, Dh)[:, :L].reshape(B, S, H, Dh)
    m = jnp.transpose(m, (0, 1, 4, 2, 3)).reshape(B, Lp, d, H)[:, :L].reshape(B, S, H)
    l = jnp.transpose(l, (0, 1, 4, 2, 3)).reshape(B, Lp, d, H)[:, :L].reshape(B, S, H)
    return o, m, l


def _merge_by_denominator(parts):
    m_all = functools_reduce_max([m for _, m, _ in parts])
    ws = [l * jnp.exp(m - m_all) for _, m, l in parts]
    num = sum(w[..., None] * o for (o, _, _), w in zip(parts, ws))
    return num / sum(ws)[..., None]


def functools_reduce_max(arrs):
    out = arrs[0]
    for a in arrs[1:]:
        out = jnp.maximum(out, a)
    return out


def _mla_attention(q_nope, q_pe, k_nope, k_pe, v):
    B, S, H, _ = q_nope.shape
    nb = S // ATTN_BLOCK
    scale = (QK_NOPE + QK_ROPE) ** -0.5
    qn = q_nope.reshape(B, nb, ATTN_BLOCK, H, QK_NOPE).transpose(1, 0, 2, 3, 4)
    qp = q_pe.reshape(B, nb, ATTN_BLOCK, H, QK_ROPE).transpose(1, 0, 2, 3, 4)
    kpos = jnp.arange(S)

    def block(args):
        i, qn_b, qp_b = args
        s = (jnp.einsum('bqhd,bkhd->bhqk', qn_b, k_nope, preferred_element_type=jnp.float32)
             + jnp.einsum('bqhd,bkd->bhqk', qp_b, k_pe, preferred_element_type=jnp.float32)) * scale
        qpos = i * ATTN_BLOCK + jnp.arange(ATTN_BLOCK)
        s = jnp.where(kpos[None, :] <= qpos[:, None], s, NEG)
        p = jax.nn.softmax(s, axis=-1)
        return jnp.einsum('bhqk,bkhd->bqhd', p.astype(v.dtype), v)

    o = lax.map(block, (jnp.arange(nb), qn, qp))
    return o.transpose(1, 0, 2, 3, 4).reshape(B, S, H, V_HEAD)


def _hybrid_mixer(x, w_in, g_q, w_uq, g_kv, w_ukv, g_out_dil, g_out_mla, w_o,
                  cos_d, sin_d, cos_m, sin_m):
    B, S, _ = x.shape
    proj = jnp.einsum('bsd,df->bsf', x, w_in)
    q_d, k_d, v_d, c_q, c_kv, k_pe = jnp.split(proj, IN_SPLITS, axis=-1)
    q_d = _partial_rope(q_d.reshape(B, S, N_HEADS_DIL, HEAD_DIM), cos_d, sin_d)
    k_d = _partial_rope(k_d.reshape(B, S, N_HEADS_DIL, HEAD_DIM), cos_d, sin_d)
    v_d = v_d.reshape(B, S, N_HEADS_DIL, HEAD_DIM)
    parts = [_dilated_window_attention(q_d, k_d, v_d, w, r) for (w, r) in DIL_PATTERNS]
    o_dil = _merge_by_denominator(parts).astype(x.dtype).reshape(B, S, DIL_WIDTH)
    c_q = _rmsnorm(c_q, g_q)
    q = jnp.einsum('bsr,rf->bsf', c_q, w_uq).reshape(B, S, N_HEADS_MLA, QK_NOPE + QK_ROPE)
    q_nope = q[..., :QK_NOPE]
    q_pe = _apply_rope(q[..., QK_NOPE:], cos_m, sin_m)
    c_kv = _rmsnorm(c_kv, g_kv)
    kv = jnp.einsum('bsr,rf->bsf', c_kv, w_ukv).reshape(B, S, N_HEADS_MLA, QK_NOPE + V_HEAD)
    k_nope, v_m = kv[..., :QK_NOPE], kv[..., QK_NOPE:]
    k_pe = _apply_rope(k_pe[:, :, None, :], cos_m, sin_m)[:, :, 0]
    o_mla = _mla_attention(q_nope, q_pe, k_nope, k_pe, v_m).reshape(B, S, MLA_WIDTH)
    y = jnp.concatenate([_rmsnorm(o_dil, g_out_dil), _rmsnorm(o_mla, g_out_mla)], axis=-1)
    return jnp.einsum('bsf,fd->bsd', y, w_o)


def _moe_ffn(x, router_w, router_bias, w1, w3, w2, ws1, ws3, ws2):
    B, S, D = x.shape
    T = B * S
    xf = x.reshape(T, D)
    logits = jnp.einsum('td,ed->te', xf, router_w, preferred_element_type=jnp.float32)
    scores = jax.nn.sigmoid(logits)
    choice = scores + router_bias.astype(jnp.float32)[None, :]
    grp = choice.reshape(T, N_GROUP, N_EXPERTS // N_GROUP)
    grp_score = jnp.sum(lax.top_k(grp, 2)[0], axis=-1)
    _, gidx = lax.top_k(grp_score, TOPK_GROUP)
    gmask = jnp.sum(jax.nn.one_hot(gidx, N_GROUP, dtype=jnp.float32), axis=1) > 0
    emask = jnp.repeat(gmask, N_EXPERTS // N_GROUP, axis=1)
    _, eidx = lax.top_k(jnp.where(emask, choice, -jnp.inf), TOP_K)
    gw = jnp.take_along_axis(scores, eidx, axis=1)
    gw = gw / jnp.sum(gw, axis=-1, keepdims=True) * ROUTED_SCALE
    TK = T * TOP_K
    flat_e = eidx.reshape(TK)
    flat_t = jnp.repeat(jnp.arange(T, dtype=jnp.int32), TOP_K)
    flat_w = gw.reshape(TK)
    order = jnp.argsort(flat_e, stable=True)
    se, st, sw = flat_e[order], flat_t[order], flat_w[order]
    counts = jnp.bincount(flat_e, length=N_EXPERTS)
    start = jnp.cumsum(counts) - counts
    padded = (counts + EXPERT_BLOCK - 1) // EXPERT_BLOCK * EXPERT_BLOCK
    pend = jnp.cumsum(padded)
    pstart = pend - padded
    dest = pstart[se] + (jnp.arange(TK, dtype=jnp.int32) - start[se])
    NB = -(-TK // EXPERT_BLOCK) + N_EXPERTS
    R = NB * EXPERT_BLOCK
    row_tok = jnp.full((R,), T, dtype=jnp.int32).at[dest].set(st)
    row_w = jnp.zeros((R,), jnp.float32).at[dest].set(sw)
    block_e = jnp.minimum(jnp.searchsorted(pend, jnp.arange(NB) * EXPERT_BLOCK, side='right'),
                          N_EXPERTS - 1)
    xpad = jnp.concatenate([xf, jnp.zeros((1, D), xf.dtype)], axis=0)

    def body(b, acc):
        tok = lax.dynamic_slice(row_tok, (b * EXPERT_BLOCK,), (EXPERT_BLOCK,))
        wb = lax.dynamic_slice(row_w, (b * EXPERT_BLOCK,), (EXPERT_BLOCK,))
        e = block_e[b]
        xb = xpad[tok]
        hb = jax.nn.silu(xb @ w1[e]) * (xb @ w3[e])
        yb = (hb @ w2[e]) * wb.astype(xb.dtype)[:, None]
        return acc.at[tok].add(yb.astype(acc.dtype))

    routed = lax.fori_loop(0, NB, body, jnp.zeros((T + 1, D), xf.dtype))[:T]
    shared = (jax.nn.silu(xf @ ws1) * (xf @ ws3)) @ ws2
    return (routed + shared).reshape(B, S, D)


def setup_inputs(seed: int = 0) -> dict:
    key = jax.random.key(seed)
    ks = jax.random.split(key, 24)
    L = DEPTH
    f32 = jnp.float32

    def nrm(k, shape, scale):
        return jax.random.normal(k, shape, f32) * scale

    def gain(k, shape):
        return 1.0 + 0.02 * jax.random.normal(k, shape, f32)

    return {
        "x": nrm(ks[0], (BATCH, SEQ, D_MODEL), 1.0),
        "w_in": nrm(ks[1], (L, D_MODEL, IN_WIDTH), D_MODEL ** -0.5),
        "g_q": gain(ks[2], (L, Q_LORA)),
        "w_uq": nrm(ks[3], (L, Q_LORA, N_HEADS_MLA * (QK_NOPE + QK_ROPE)), Q_LORA ** -0.5),
        "g_kv": gain(ks[4], (L, KV_LORA)),
        "w_ukv": nrm(ks[5], (L, KV_LORA, N_HEADS_MLA * (QK_NOPE + V_HEAD)), KV_LORA ** -0.5),
        "g_out_dil": gain(ks[6], (L, DIL_WIDTH)),
        "g_out_mla": gain(ks[7], (L, MLA_WIDTH)),
        "w_o": nrm(ks[8], (L, MIX_WIDTH, D_MODEL), MIX_WIDTH ** -0.5 * DEEPNORM_BETA),
        "ln1_g": gain(ks[9], (L, D_MODEL)),
        "ln1_b": nrm(ks[10], (L, D_MODEL), 0.02),
        "router_w": nrm(ks[11], (L, N_EXPERTS, D_MODEL), D_MODEL ** -0.5),
        "router_bias": nrm(ks[12], (L, N_EXPERTS), 0.01),
        "w1": nrm(ks[13], (L, N_EXPERTS, D_MODEL, EXPERT_FF), D_MODEL ** -0.5),
        "w3": nrm(ks[14], (L, N_EXPERTS, D_MODEL, EXPERT_FF), D_MODEL ** -0.5),
        "w2": nrm(ks[15], (L, N_EXPERTS, EXPERT_FF, D_MODEL), EXPERT_FF ** -0.5 * DEEPNORM_BETA),
        "ws1": nrm(ks[16], (L, D_MODEL, SHARED_FF), D_MODEL ** -0.5),
        "ws3": nrm(ks[17], (L, D_MODEL, SHARED_FF), D_MODEL ** -0.5),
        "ws2": nrm(ks[18], (L, SHARED_FF, D_MODEL), SHARED_FF ** -0.5 * DEEPNORM_BETA),
        "ln2_g": gain(ks[19], (L, D_MODEL)),
        "ln2_b": nrm(ks[20], (L, D_MODEL), 0.02),
    }


def reference(x, w_in, g_q, w_uq, g_kv, w_ukv, g_out_dil, g_out_mla, w_o, ln1_g, ln1_b,
              router_w, router_bias, w1, w3, w2, ws1, ws3, ws2, ln2_g, ln2_b):
    S = x.shape[1]
    cos_d, sin_d = _rope_table(S, ROPE_DIM_DIL)
    cos_m, sin_m = _rope_table(S, QK_ROPE)
    h = x
    for layer in range(DEPTH):
        mix = _hybrid_mixer(h, w_in[layer], g_q[layer], w_uq[layer], g_kv[layer], w_ukv[layer],
                            g_out_dil[layer], g_out_mla[layer], w_o[layer],
                            cos_d, sin_d, cos_m, sin_m)
        h = _layernorm(DEEPNORM_ALPHA * h + mix, ln1_g[layer], ln1_b[layer])
        ffn = _moe_ffn(h, router_w[layer], router_bias[layer], w1[layer], w3[layer], w2[layer],
                       ws1[layer], ws3[layer], ws2[layer])
        h = _layernorm(DEEPNORM_ALPHA * h + ffn, ln2_g[layer], ln2_b[layer])
    return h
```

```python
import functools

import jax
import jax.numpy as jnp
from jax import lax
from jax.experimental import pallas as pl
from jax.experimental.pallas import tpu as pltpu

D_MODEL = 1024
HEAD_DIM = 64
N_HEADS_DIL = 8
DIL_PATTERNS = ((128, 1), (512, 4), (2048, 16))
ROPE_DIM_DIL = HEAD_DIM // 4
ROPE_THETA = 500000.0
ATTN_BLOCK = 128
N_HEADS_MLA = 8
Q_LORA = 384
KV_LORA = 128
QK_NOPE = 64
QK_ROPE = 32
V_HEAD = 64
DIL_WIDTH = N_HEADS_DIL * HEAD_DIM
MLA_WIDTH = N_HEADS_MLA * V_HEAD
N_EXPERTS = 256
TOP_K = 8
N_GROUP = 8
TOPK_GROUP = 4
GROUP_SIZE = N_EXPERTS // N_GROUP
EXPERT_FF = 256
SHARED_FF = 256
ROUTED_SCALE = 2.5
DEPTH = 1
DEEPNORM_ALPHA = (2.0 * DEPTH) ** 0.25
LN_EPS = 1e-5
RMS_EPS = 1e-6
NEG = -1e30

LANES = 128
SUBLANES = 8
VMEM_LIMIT = 56 * 1024 * 1024

PROJ_TM = 512
DIL_TQ = 2048
MLA_TQ = 512
MLA_TK = 512
POST_TM = 512
ROUTE_TT = 512
DISPATCH_TT = 512
MOE_BM = 256
COMBINE_TT = 128

MLA_HEAD_PAD = LANES

bf16 = jnp.bfloat16
f32 = jnp.float32


def _dot(a, b):
    return jnp.dot(a, b, preferred_element_type=f32)


def _dot_nt(a, b):
    return lax.dot_general(a, b, (((1,), (1,)), ((), ())), preferred_element_type=f32)


def _rope_lanes(x, tab_ref, shift):
    return (x * tab_ref[0]
            + pltpu.roll(x, shift, 1) * tab_ref[1]
            + pltpu.roll(x, LANES - shift, 1) * tab_ref[2])


def _rms(x, g):
    return x * lax.rsqrt(jnp.mean(x * x, axis=-1, keepdims=True) + RMS_EPS) * g


def _layernorm(x, g, b):
    mu = jnp.mean(x, axis=-1, keepdims=True)
    xc = x - mu
    var = jnp.mean(xc * xc, axis=-1, keepdims=True)
    return xc * lax.rsqrt(var + LN_EPS) * g + b


def _silu(x):
    return x * (1.0 / (1.0 + jnp.exp(-x)))


def _proj_kernel(x_ref, win_ref, wuq_ref, wukv_ref, gq_ref, gkv_ref, ropd_ref, ropm_ref,
                 qd_ref, kd_ref, vd_ref, qm_ref, km_ref, vm_ref):
    xb = x_ref[...].astype(bf16)
    o_k, o_v, o_cq, o_ckv, o_kpe = DIL_WIDTH, 2 * DIL_WIDTH, 3 * DIL_WIDTH, 3 * DIL_WIDTH + Q_LORA, \
        3 * DIL_WIDTH + Q_LORA + KV_LORA
    rope_shift_d = ROPE_DIM_DIL // 2
    rope_shift_m = QK_ROPE // 2

    q = _dot(xb, win_ref[:, 0:o_k])
    k = _dot(xb, win_ref[:, o_k:o_v])
    for j in range(DIL_WIDTH // LANES):
        sl = slice(j * LANES, (j + 1) * LANES)
        qd_ref[:, sl] = _rope_lanes(q[:, sl], ropd_ref, rope_shift_d) * (HEAD_DIM ** -0.5)
        kd_ref[:, sl] = _rope_lanes(k[:, sl], ropd_ref, rope_shift_d)
    vd_ref[...] = _dot(xb, win_ref[:, o_v:o_cq])

    cq = _rms(_dot(xb, win_ref[:, o_cq:o_ckv]), gq_ref[...]).astype(bf16)
    qm = _dot(cq, wuq_ref[...])
    ckv = _rms(_dot(xb, win_ref[:, o_ckv:o_kpe]), gkv_ref[...]).astype(bf16)
    kn = _dot(ckv, wukv_ref[:, 0:N_HEADS_MLA * MLA_HEAD_PAD])
    kpe = _rope_lanes(_dot(xb, win_ref[:, o_kpe:o_kpe + LANES]), ropm_ref, rope_shift_m)
    scale = (QK_NOPE + QK_ROPE) ** -0.5
    for h in range(N_HEADS_MLA):
        sl = slice(h * LANES, (h + 1) * LANES)
        qm_ref[:, sl] = (_rope_lanes(qm[:, sl], ropm_ref, rope_shift_m) * scale).astype(bf16)
        km_ref[:, sl] = (kn[:, sl] + kpe).astype(bf16)
    vm_ref[...] = _dot(ckv, wukv_ref[:, N_HEADS_MLA * MLA_HEAD_PAD:]).astype(bf16)


def _rope_tables(seq):
    pos = jnp.arange(seq, dtype=f32)[:, None]
    lane = jnp.arange(LANES)

    def build(dim, lane_in_head, period):
        half = dim // 2
        inv = ROPE_THETA ** (-jnp.arange(0, dim, 2, dtype=f32) / dim)
        ang = pos * inv[None, :]
        cos, sin = jnp.cos(ang), jnp.sin(ang)
        off = lane_in_head(lane % period)
        in_x1 = (off >= 0) & (off < half)
        in_x2 = (off >= half) & (off < dim)
        idx = jnp.clip(jnp.where(in_x2, off - half, off), 0, half - 1)
        c = jnp.where((in_x1 | in_x2)[None, :], cos[:, idx], 1.0)
        sa = jnp.where(in_x2[None, :], sin[:, idx], 0.0)
        sb = jnp.where(in_x1[None, :], -sin[:, idx], 0.0)
        return jnp.stack([c, sa, sb]).astype(f32)

    tab_d = build(ROPE_DIM_DIL, lambda l: jnp.where(l < ROPE_DIM_DIL, l, -1), HEAD_DIM)
    tab_m = build(QK_ROPE, lambda l: jnp.where((l >= QK_NOPE) & (l < QK_NOPE + QK_ROPE), l - QK_NOPE, -1), LANES)
    return tab_d, tab_m


def _project(x2d, w_in, w_uq, w_ukv, g_q, g_kv, seq):
    T = x2d.shape[0]
    tm = PROJ_TM
    n_seq_tiles = seq // tm
    tab_d, tab_m = _rope_tables(seq)

    kpe_cols = jnp.zeros((D_MODEL, LANES), f32).at[:, QK_NOPE:QK_NOPE + QK_ROPE].set(w_in[:, -QK_ROPE:])
    w_in_b = jnp.concatenate([w_in[:, :-QK_ROPE], kpe_cols], axis=1).astype(bf16)
    w_uq_b = jnp.pad(w_uq.reshape(Q_LORA, N_HEADS_MLA, QK_NOPE + QK_ROPE),
                     ((0, 0), (0, 0), (0, MLA_HEAD_PAD - QK_NOPE - QK_ROPE))
                     ).reshape(Q_LORA, N_HEADS_MLA * MLA_HEAD_PAD).astype(bf16)
    w_ukv3 = w_ukv.reshape(KV_LORA, N_HEADS_MLA, QK_NOPE + V_HEAD)
    w_uk = jnp.pad(w_ukv3[:, :, :QK_NOPE], ((0, 0), (0, 0), (0, MLA_HEAD_PAD - QK_NOPE))
                   ).reshape(KV_LORA, N_HEADS_MLA * MLA_HEAD_PAD)
    w_uv = w_ukv3[:, :, QK_NOPE:].reshape(KV_LORA, MLA_WIDTH)
    w_ukv_b = jnp.concatenate([w_uk, w_uv], axis=1).astype(bf16)

    full = lambda a: pl.BlockSpec(a.shape, lambda i: (0,) * a.ndim)
    rows = lambda w: pl.BlockSpec((tm, w), lambda i: (i, 0))
    tab_spec = pl.BlockSpec((3, tm, LANES), lambda i: (0, i % n_seq_tiles, 0))
    gq2, gkv2 = g_q.reshape(1, Q_LORA), g_kv.reshape(1, KV_LORA)
    mla_w = N_HEADS_MLA * MLA_HEAD_PAD
    return pl.pallas_call(
        _proj_kernel,
        grid=(T // tm,),
        in_specs=[rows(D_MODEL), full(w_in_b), full(w_uq_b), full(w_ukv_b), full(gq2), full(gkv2),
                  tab_spec, tab_spec],
        out_specs=[rows(DIL_WIDTH), rows(DIL_WIDTH), rows(DIL_WIDTH), rows(mla_w), rows(mla_w), rows(MLA_WIDTH)],
        out_shape=[jax.ShapeDtypeStruct((T, DIL_WIDTH), f32)] * 3
        + [jax.ShapeDtypeStruct((T, mla_w), bf16)] * 2 + [jax.ShapeDtypeStruct((T, MLA_WIDTH), bf16)],
        compiler_params=pltpu.CompilerParams(dimension_semantics=("parallel",), vmem_limit_bytes=VMEM_LIMIT),
        name="proj",
    )(x2d, w_in_b, w_uq_b, w_ukv_b, gq2, gkv2, tab_d, tab_m)


def _dilated_kernel(q_ref, kc_ref, kp_ref, vc_ref, vp_ref, o_ref, acc_ref, m_ref, l_ref):
    blk = ATTN_BLOCK
    has_prev_tile = pl.program_id(2) > 0
    lane = lax.broadcasted_iota(jnp.int32, (1, LANES), 1)
    head0 = lane < HEAD_DIM
    qi = lax.broadcasted_iota(jnp.int32, (blk, blk), 0)
    ki = lax.broadcasted_iota(jnp.int32, (blk, blk), 1)
    cur_ok = ki <= qi
    prev_ok = ki >= qi

    first = True
    for window, d in DIL_PATTERNS:
        assert window // d == blk and DIL_TQ % (d * blk) == 0
        n_blk = DIL_TQ // (d * blk)
        for r in range(d):
            for b in range(n_blk):
                rows = pl.ds(r + d * blk * b, blk, stride=d) if d > 1 else pl.ds(blk * b, blk)
                if b > 0:
                    prow = (pl.ds(r + d * blk * (b - 1), blk, stride=d) if d > 1
                            else pl.ds(blk * (b - 1), blk))
                    kp, vp, p_ok = kc_ref[0, prow, :], vc_ref[0, prow, :], prev_ok
                else:
                    prow = (pl.ds(r + d * blk * (n_blk - 1), blk, stride=d) if d > 1
                            else pl.ds(blk * (n_blk - 1), blk))
                    kp, vp = kp_ref[0, prow, :], vp_ref[0, prow, :]
                    p_ok = prev_ok & has_prev_tile
                q = q_ref[0, rows, :]
                kc = kc_ref[0, rows, :].astype(bf16)
                vc = vc_ref[0, rows, :].astype(bf16)
                kp = kp.astype(bf16)
                vp = vp.astype(bf16)
                pv = []
                alphas = []
                for h in range(2):
                    qh = jnp.where(head0 if h == 0 else ~head0, q, 0.0).astype(bf16)
                    sc = jnp.where(cur_ok, _dot_nt(qh, kc), NEG)
                    sp = jnp.where(p_ok, _dot_nt(qh, kp), NEG)
                    mb = jnp.maximum(jnp.max(sc, axis=1, keepdims=True), jnp.max(sp, axis=1, keepdims=True))
                    if first:
                        m_new = mb
                    else:
                        m_old = m_ref[h, rows, :]
                        m_new = jnp.maximum(m_old, mb)
                        alpha = jnp.exp(m_old - m_new)
                    pc = jnp.exp(sc - m_new)
                    pp = jnp.exp(sp - m_new)
                    lb = jnp.sum(pc, axis=1, keepdims=True) + jnp.sum(pp, axis=1, keepdims=True)
                    if first:
                        l_ref[h, rows, :] = lb
                    else:
                        l_ref[h, rows, :] = l_ref[h, rows, :] * alpha + lb
                        alphas.append(alpha)
                    m_ref[h, rows, :] = m_new
                    pv.append(_dot(pc.astype(bf16), vc) + _dot(pp.astype(bf16), vp))
                upd = jnp.where(head0, pv[0], pv[1])
                if first:
                    acc_ref[rows, :] = upd
                else:
                    acc_ref[rows, :] = acc_ref[rows, :] * jnp.where(head0, alphas[0], alphas[1]) + upd
        first = False

    inv_l = jnp.where(head0, 1.0 / l_ref[0], 1.0 / l_ref[1])
    o_ref[0] = (acc_ref[...] * inv_l).astype(o_ref.dtype)


def _dilated_attention(qd, kd, vd, batch, seq):
    tq = DIL_TQ
    q3, k3, v3 = (a.reshape(batch, seq, DIL_WIDTH) for a in (qd, kd, vd))
    cur = pl.BlockSpec((1, tq, LANES), lambda b, hp, i: (b, i, hp))
    prev = pl.BlockSpec((1, tq, LANES), lambda b, hp, i: (b, jnp.maximum(i - 1, 0), hp))
    out = pl.pallas_call(
        _dilated_kernel,
        grid=(batch, DIL_WIDTH // LANES, seq // tq),
        in_specs=[cur, cur, prev, cur, prev],
        out_specs=cur,
        out_shape=jax.ShapeDtypeStruct((batch, seq, DIL_WIDTH), bf16),
        scratch_shapes=[pltpu.VMEM((tq, LANES), f32), pltpu.VMEM((2, tq, 1), f32), pltpu.VMEM((2, tq, 1), f32)],
        compiler_params=pltpu.CompilerParams(dimension_semantics=("parallel", "parallel", "arbitrary"),
                                             vmem_limit_bytes=VMEM_LIMIT),
        name="dilated_attn",
    )(q3, k3, k3, v3, v3)
    return out.reshape(batch * seq, DIL_WIDTH)


def _mla_kernel(q_ref, k_ref, v_ref, o_ref, acc_ref, m_ref, l_ref):
    tq, tk = MLA_TQ, MLA_TK
    i = pl.program_id(2)
    lane = lax.broadcasted_iota(jnp.int32, (1, LANES), 1)
    head0 = lane < V_HEAD
    causal = lax.broadcasted_iota(jnp.int32, (tq, tk), 1) <= lax.broadcasted_iota(jnp.int32, (tq, tk), 0)

    for h in range(2):
        hs = slice(h * LANES, (h + 1) * LANES)
        qh = q_ref[0, :, hs]
        m_ref[h] = jnp.full((tq, 1), NEG, f32)
        l_ref[h] = jnp.zeros((tq, 1), f32)
        acc_ref[h] = jnp.zeros((tq, LANES), f32)

        def step(j, masked):
            rows = pl.ds(pl.multiple_of(j * tk, tk), tk)
            s = _dot_nt(qh, k_ref[0, rows, hs])
            if masked:
                s = jnp.where(causal, s, NEG)
            m_old = m_ref[h]
            m_new = jnp.maximum(m_old, jnp.max(s, axis=1, keepdims=True))
            alpha = jnp.exp(m_old - m_new)
            p = jnp.exp(s - m_new)
            l_ref[h] = l_ref[h] * alpha + jnp.sum(p, axis=1, keepdims=True)
            acc_ref[h] = acc_ref[h] * alpha + _dot(p.astype(bf16), v_ref[0, rows, :])
            m_ref[h] = m_new

        def body(j, carry):
            step(j, False)
            return carry

        lax.fori_loop(0, i, body, 0)
        step(i, True)

    out = jnp.where(head0, acc_ref[0] * (1.0 / l_ref[0]), acc_ref[1] * (1.0 / l_ref[1]))
    o_ref[0] = out.astype(o_ref.dtype)


def _mla_attention(qm, km, vm, batch, seq):
    assert MLA_TQ == MLA_TK
    tq = MLA_TQ
    mla_w = N_HEADS_MLA * MLA_HEAD_PAD
    q3 = qm.reshape(batch, seq, mla_w)
    k3 = km.reshape(batch, seq, mla_w)
    v3 = vm.reshape(batch, seq, MLA_WIDTH)
    out = pl.pallas_call(
        _mla_kernel,
        grid=(batch, N_HEADS_MLA // 2, seq // tq),
        in_specs=[pl.BlockSpec((1, tq, 2 * LANES), lambda b, hp, i: (b, i, hp)),
                  pl.BlockSpec((1, seq, 2 * LANES), lambda b, hp, i: (b, 0, hp)),
                  pl.BlockSpec((1, seq, LANES), lambda b, hp, i: (b, 0, hp))],
        out_specs=pl.BlockSpec((1, tq, LANES), lambda b, hp, i: (b, i, hp)),
        out_shape=jax.ShapeDtypeStruct((batch, seq, MLA_WIDTH), bf16),
        scratch_shapes=[pltpu.VMEM((2, tq, LANES), f32), pltpu.VMEM((2, tq, 1), f32), pltpu.VMEM((2, tq, 1), f32)],
        compiler_params=pltpu.CompilerParams(dimension_semantics=("parallel", "parallel", "arbitrary"),
                                             vmem_limit_bytes=VMEM_LIMIT),
        name="mla_attn",
    )(q3, k3, v3)
    return out.reshape(batch * seq, MLA_WIDTH)


def _post_attn_kernel(x_ref, od_ref, om_ref, gd_ref, gm_ref, wo_ref, g1_ref, b1_ref, rw_ref,
                      ws1_ref, ws3_ref, ws2_ref, base_ref, hrow_ref, logit_ref):
    tm = POST_TM
    yd = _rms(od_ref[...].astype(f32), gd_ref[...]).astype(bf16)
    ym = _rms(om_ref[...].astype(f32), gm_ref[...]).astype(bf16)
    mix = _dot(yd, wo_ref[0:DIL_WIDTH, :]) + _dot(ym, wo_ref[DIL_WIDTH:, :])
    h1 = _layernorm(DEEPNORM_ALPHA * x_ref[...] + mix, g1_ref[...], b1_ref[...])
    for s in range(D_MODEL // LANES):
        hrow_ref[pl.ds(s, tm, stride=SUBLANES), :] = h1[:, s * LANES:(s + 1) * LANES]
    hb = h1.astype(bf16)
    logit_ref[...] = _dot_nt(rw_ref[...], hb)
    act = (_silu(_dot(hb, ws1_ref[...])) * _dot(hb, ws3_ref[...])).astype(bf16)
    base_ref[...] = DEEPNORM_ALPHA * h1 + _dot(act, ws2_ref[...])


def _post_attention(x2d, o_dil, o_mla, g_out_dil, g_out_mla, w_o, ln1_g, ln1_b, router_w, ws1, ws3, ws2):
    T = x2d.shape[0]
    tm = POST_TM
    full = lambda a: pl.BlockSpec(a.shape, lambda i: (0,) * a.ndim)
    rows = lambda w: pl.BlockSpec((tm, w), lambda i: (i, 0))
    args = (x2d, o_dil, o_mla, g_out_dil.reshape(1, -1), g_out_mla.reshape(1, -1), w_o.astype(bf16),
            ln1_g.reshape(1, -1), ln1_b.reshape(1, -1), router_w.astype(bf16),
            ws1.astype(bf16), ws3.astype(bf16), ws2.astype(bf16))
    return pl.pallas_call(
        _post_attn_kernel,
        grid=(T // tm,),
        in_specs=[rows(D_MODEL), rows(DIL_WIDTH), rows(MLA_WIDTH)] + [full(a) for a in args[3:]],
        out_specs=[rows(D_MODEL), pl.BlockSpec((tm * SUBLANES, LANES), lambda i: (i, 0)),
                   pl.BlockSpec((N_EXPERTS, tm), lambda i: (0, i))],
        out_shape=[jax.ShapeDtypeStruct((T, D_MODEL), f32),
                   jax.ShapeDtypeStruct((T * SUBLANES, LANES), f32),
                   jax.ShapeDtypeStruct((N_EXPERTS, T), f32)],
        compiler_params=pltpu.CompilerParams(dimension_semantics=("parallel",), vmem_limit_bytes=VMEM_LIMIT),
        name="post_attn",
    )(*args)


def _first_argmax(vals, row_ids, n_rows):
    mx = jnp.max(vals, axis=0, keepdims=True)
    idx = jnp.min(jnp.where(vals == mx, row_ids, n_rows), axis=0, keepdims=True)
    return mx, idx


def _router_kernel(logit_ref, bias_ref, e_ref, rank_ref, gw_ref, count_ref, carry_ref):
    tt = ROUTE_TT
    step = pl.program_id(0)

    @pl.when(step == 0)
    def _():
        carry_ref[...] = jnp.zeros_like(carry_ref)

    scores = 1.0 / (1.0 + jnp.exp(-logit_ref[...]))
    choice = scores + bias_ref[...]
    row = lax.broadcasted_iota(jnp.int32, (N_EXPERTS, tt), 0)
    grow = lax.broadcasted_iota(jnp.int32, (GROUP_SIZE, tt), 0)
    neg_inf = -jnp.inf

    gscore = []
    for g in range(N_GROUP):
        cg = choice[g * GROUP_SIZE:(g + 1) * GROUP_SIZE, :]
        m1, i1 = _first_argmax(cg, grow, GROUP_SIZE)
        m2 = jnp.max(jnp.where(grow == i1, neg_inf, cg), axis=0, keepdims=True)
        gscore.append(m1 + m2)
    masked = []
    for g in range(N_GROUP):
        beaten = jnp.zeros((1, tt), jnp.int32)
        for o in range(N_GROUP):
            if o == g:
                continue
            wins = (gscore[o] >= gscore[g]) if o < g else (gscore[o] > gscore[g])
            beaten = beaten + wins.astype(jnp.int32)
        keep = beaten < TOPK_GROUP
        masked.append(jnp.where(keep, choice[g * GROUP_SIZE:(g + 1) * GROUP_SIZE, :], neg_inf))
    cur = jnp.concatenate(masked, axis=0)

    sel_idx, sel_score = [], []
    for _ in range(TOP_K):
        _, idx = _first_argmax(cur, row, N_EXPERTS)
        hit = row == idx
        sel_idx.append(idx)
        sel_score.append(jnp.sum(jnp.where(hit, scores, 0.0), axis=0, keepdims=True))
        cur = jnp.where(hit, neg_inf, cur)
    onehot = jnp.zeros((N_EXPERTS, tt), f32)
    for idx in sel_idx:
        onehot = onehot + (row == idx).astype(f32)

    upper = (lax.broadcasted_iota(jnp.int32, (tt, tt), 0) < lax.broadcasted_iota(jnp.int32, (tt, tt), 1))
    before = _dot(onehot.astype(bf16), upper.astype(bf16)) + carry_ref[...]
    carry_ref[...] = carry_ref[...] + jnp.sum(onehot, axis=1, keepdims=True)
    count_ref[...] = carry_ref[...].astype(jnp.int32)

    denom = sel_score[0]
    for s in sel_score[1:]:
        denom = denom + s
    for k in range(TOP_K):
        hit = row == sel_idx[k]
        e_ref[k:k + 1, :] = sel_idx[k]
        rank_ref[k:k + 1, :] = jnp.sum(jnp.where(hit, before, 0.0), axis=0, keepdims=True).astype(jnp.int32)
        gw_ref[k:k + 1, :] = sel_score[k] / denom * ROUTED_SCALE


def _route(logits_t, router_bias):
    T = logits_t.shape[1]
    tt = ROUTE_TT
    slot = pl.BlockSpec((TOP_K, tt), lambda i: (0, i))
    col = pl.BlockSpec((N_EXPERTS, 1), lambda i: (0, 0))
    return pl.pallas_call(
        _router_kernel,
        grid=(T // tt,),
        in_specs=[pl.BlockSpec((N_EXPERTS, tt), lambda i: (0, i)), col],
        out_specs=[slot, slot, slot, col],
        out_shape=[jax.ShapeDtypeStruct((TOP_K, T), jnp.int32), jax.ShapeDtypeStruct((TOP_K, T), jnp.int32),
                   jax.ShapeDtypeStruct((TOP_K, T), f32), jax.ShapeDtypeStruct((N_EXPERTS, 1), jnp.int32)],
        scratch_shapes=[pltpu.VMEM((N_EXPERTS, 1), f32)],
        compiler_params=pltpu.CompilerParams(dimension_semantics=("arbitrary",), vmem_limit_bytes=VMEM_LIMIT),
        name="router",
    )(logits_t, router_bias.reshape(N_EXPERTS, 1))


def _token_tile(ref, idx):
    return ref.at[pl.ds(pl.multiple_of(idx * SUBLANES, SUBLANES), SUBLANES), :]


def _dispatch_kernel(start_ref, e_ref, rank_ref, h_hbm, xs_hbm, sem):
    tt = DISPATCH_TT
    base = pl.program_id(0) * tt

    def issue(j, carry):
        src = _token_tile(h_hbm, base + j)
        for k in range(TOP_K):
            dest = start_ref[e_ref[k, j]] + rank_ref[k, j]
            pltpu.make_async_copy(src, _token_tile(xs_hbm, dest), sem).start()
        return carry

    lax.fori_loop(0, tt, issue, 0)
    n_rows = tt * TOP_K * SUBLANES
    pltpu.make_async_copy(h_hbm.at[pl.ds(0, n_rows), :], xs_hbm.at[pl.ds(0, n_rows), :], sem).wait()


def _dispatch(start, e_t, rank_t, h_rows):
    T = e_t.shape[1]
    tt = DISPATCH_TT
    slot = pl.BlockSpec((TOP_K, tt), lambda i, *_: (0, i), memory_space=pltpu.SMEM)
    return pl.pallas_call(
        _dispatch_kernel,
        grid_spec=pltpu.PrefetchScalarGridSpec(
            num_scalar_prefetch=1,
            grid=(T // tt,),
            in_specs=[slot, slot, pl.BlockSpec(memory_space=pl.ANY)],
            out_specs=pl.BlockSpec(memory_space=pl.ANY),
            scratch_shapes=[pltpu.SemaphoreType.DMA(())],
        ),
        out_shape=jax.ShapeDtypeStruct((T * TOP_K * SUBLANES, LANES), f32),
        compiler_params=pltpu.CompilerParams(dimension_semantics=("arbitrary",)),
        name="dispatch",
    )(start, e_t, rank_t, h_rows)


def _moe_kernel(tile_ref, exp_ref, valid_ref, start_ref, end_ref, x_ref, w1_ref, w3_ref, w2_ref, y_ref,
                xb_ref, w1b_ref, w3b_ref, w2b_ref):
    bm = MOE_BM
    i = pl.program_id(0)
    prev = jnp.maximum(i - 1, 0)
    e, tile = exp_ref[i], tile_ref[i]
    new_expert = (i == 0) | (e != exp_ref[prev])
    new_tile = (i == 0) | (tile != tile_ref[prev])

    @pl.when(new_expert)
    def _():
        w1b_ref[...] = w1_ref[...].astype(bf16)
        w3b_ref[...] = w3_ref[...].astype(bf16)
        w2b_ref[...] = w2_ref[...].astype(bf16)

    @pl.when(new_tile)
    def _():
        for s in range(D_MODEL // LANES):
            xb_ref[:, s * LANES:(s + 1) * LANES] = x_ref[pl.ds(s, bm, stride=SUBLANES), :].astype(bf16)
        y_ref[...] = jnp.zeros_like(y_ref)

    @pl.when(valid_ref[i] == 1)
    def _():
        xb = xb_ref[...]
        act = (_silu(_dot(xb, w1b_ref[...])) * _dot(xb, w3b_ref[...])).astype(bf16)
        y = _dot(act, w2b_ref[...])
        row = tile * bm + lax.broadcasted_iota(jnp.int32, (bm, 1), 0)
        mine = (row >= start_ref[e]) & (row < end_ref[e])
        for s in range(D_MODEL // LANES):
            rows = pl.ds(s, bm, stride=SUBLANES)
            y_ref[rows, :] = jnp.where(mine, y[:, s * LANES:(s + 1) * LANES], y_ref[rows, :])


def _work_items(counts, n_rows):
    bm = MOE_BM
    n_tiles = n_rows // bm
    n_items = n_tiles + N_EXPERTS
    end = jnp.cumsum(counts)
    start = end - counts
    first_tile = start // bm
    last_tile = jnp.maximum(end - 1, 0) // bm
    per_expert = jnp.where(counts > 0, last_tile - first_tile + 1, 0)
    item_end = jnp.cumsum(per_expert)
    item_start = item_end - per_expert
    ids = jnp.arange(n_items, dtype=jnp.int32)
    used = item_end[-1]
    exp_of = jnp.minimum(jnp.searchsorted(item_end, ids, side="right"), N_EXPERTS - 1).astype(jnp.int32)
    tile_of = (first_tile[exp_of] + ids - item_start[exp_of]).astype(jnp.int32)
    valid = ids < used
    last = jnp.maximum(used - 1, 0)
    exp_of = jnp.where(valid, exp_of, exp_of[last])
    tile_of = jnp.where(valid, tile_of, tile_of[last])
    return tile_of, exp_of, valid.astype(jnp.int32), start.astype(jnp.int32), end.astype(jnp.int32)


def _experts(items, xs, w1, w3, w2):
    bm = MOE_BM
    tile_of, exp_of, valid, start, end = items
    n_items = tile_of.shape[0]
    rows = pl.BlockSpec((bm * SUBLANES, LANES), lambda i, t, e, *_: (t[i], 0))
    wspec = lambda a: pl.BlockSpec((None, None) + a.shape[2:], lambda i, t, e, *_: (0, e[i], 0, 0))
    return pl.pallas_call(
        _moe_kernel,
        grid_spec=pltpu.PrefetchScalarGridSpec(
            num_scalar_prefetch=5,
            grid=(n_items,),
            in_specs=[rows, wspec(w1), wspec(w3), wspec(w2)],
            out_specs=rows,
            scratch_shapes=[pltpu.VMEM((bm, D_MODEL), bf16), pltpu.VMEM(w1.shape[2:], bf16),
                            pltpu.VMEM(w3.shape[2:], bf16), pltpu.VMEM(w2.shape[2:], bf16)],
        ),
        out_shape=jax.ShapeDtypeStruct(xs.shape, f32),
        compiler_params=pltpu.CompilerParams(dimension_semantics=("arbitrary",), vmem_limit_bytes=VMEM_LIMIT),
        name="experts",
    )(tile_of, exp_of, valid, start, end, xs, w1, w3, w2)


def _combine_kernel(start_ref, e_ref, rank_ref, gw_ref, y_hbm, base_ref, g2_ref, b2_ref, o_ref,
                    buf_ref, acc_ref, sem):
    tt = COMBINE_TT

    def issue(j, carry):
        for k in range(TOP_K):
            src = _token_tile(y_hbm, start_ref[e_ref[k, j]] + rank_ref[k, j])
            pltpu.make_async_copy(src, _token_tile(buf_ref.at[k], j), sem).start()
        return carry

    lax.fori_loop(0, tt, issue, 0)
    for k in range(TOP_K):
        pltpu.make_async_copy(y_hbm.at[pl.ds(0, tt * SUBLANES), :], buf_ref.at[k], sem).wait()

    def weigh(j, carry):
        rows = pl.ds(pl.multiple_of(j * SUBLANES, SUBLANES), SUBLANES)
        tot = gw_ref[0, j] * buf_ref[0, rows, :]
        for k in range(1, TOP_K):
            tot = tot + gw_ref[k, j] * buf_ref[k, rows, :]
        acc_ref[rows, :] = tot
        return carry

    lax.fori_loop(0, tt, weigh, 0)
    routed = jnp.concatenate([acc_ref[pl.ds(s, tt, stride=SUBLANES), :] for s in range(D_MODEL // LANES)], axis=1)
    o_ref[...] = _layernorm(base_ref[...] + routed, g2_ref[...], b2_ref[...])


def _combine(start, e_t, rank_t, gw_t, y_rows, base, ln2_g, ln2_b):
    T = base.shape[0]
    tt = COMBINE_TT
    slot = pl.BlockSpec((TOP_K, tt), lambda i, *_: (0, i), memory_space=pltpu.SMEM)
    vec = pl.BlockSpec((1, D_MODEL), lambda i, *_: (0, 0))
    rows = pl.BlockSpec((tt, D_MODEL), lambda i, *_: (i, 0))
    return pl.pallas_call(
        _combine_kernel,
        grid_spec=pltpu.PrefetchScalarGridSpec(
            num_scalar_prefetch=1,
            grid=(T // tt,),
            in_specs=[slot, slot, slot, pl.BlockSpec(memory_space=pl.ANY), rows, vec, vec],
            out_specs=rows,
            scratch_shapes=[pltpu.VMEM((TOP_K, tt * SUBLANES, LANES), f32), pltpu.VMEM((tt * SUBLANES, LANES), f32),
                            pltpu.SemaphoreType.DMA(())],
        ),
        out_shape=jax.ShapeDtypeStruct((T, D_MODEL), f32),
        compiler_params=pltpu.CompilerParams(dimension_semantics=("arbitrary",), vmem_limit_bytes=VMEM_LIMIT),
        name="combine",
    )(start, e_t, rank_t, gw_t, y_rows, base, ln2_g.reshape(1, -1), ln2_b.reshape(1, -1))


def _layer(x, w_in, g_q, w_uq, g_kv, w_ukv, g_out_dil, g_out_mla, w_o, ln1_g, ln1_b,
           router_w, router_bias, w1, w3, w2, ws1, ws3, ws2, ln2_g, ln2_b, layer):
    batch, seq, _ = x.shape
    T = batch * seq
    x2d = x.reshape(T, D_MODEL)
    qd, kd, vd, qm, km, vm = _project(x2d, w_in[layer], w_uq[layer], w_ukv[layer], g_q[layer], g_kv[layer], seq)
    o_dil = _dilated_attention(qd, kd, vd, batch, seq)
    o_mla = _mla_attention(qm, km, vm, batch, seq)
    base, h_rows, logits_t = _post_attention(
        x2d, o_dil, o_mla, g_out_dil[layer], g_out_mla[layer], w_o[layer], ln1_g[layer], ln1_b[layer],
        router_w[layer], ws1[layer], ws3[layer], ws2[layer])
    e_t, rank_t, gw_t, counts = _route(logits_t, router_bias[layer])
    items = _work_items(counts.reshape(N_EXPERTS), T * TOP_K)
    start = items[3]
    xs = _dispatch(start, e_t, rank_t, h_rows)
    y_rows = _experts(items, xs, w1[layer:layer + 1], w3[layer:layer + 1], w2[layer:layer + 1])
    out = _combine(start, e_t, rank_t, gw_t, y_rows, base, ln2_g[layer], ln2_b[layer])
    return out.reshape(batch, seq, D_MODEL)


def kernel(x, w_in, g_q, w_uq, g_kv, w_ukv, g_out_dil, g_out_mla, w_o, ln1_g, ln1_b, router_w, router_bias,
           w1, w3, w2, ws1, ws3, ws2, ln2_g, ln2_b):
    assert x.shape[1] % DIL_TQ == 0 and x.shape[2] == D_MODEL
    h = x
    for layer in range(w_in.shape[0]):
        h = _layer(h, w_in, g_q, w_uq, g_kv, w_ukv, g_out_dil, g_out_mla, w_o, ln1_g, ln1_b,
                   router_w, router_bias, w1, w3, w2, ws1, ws3, ws2, ln2_g, ln2_b, layer)
    return h
```

```python
import functools

import jax
import jax.numpy as jnp
from jax import lax
from jax.experimental import pallas as pl
from jax.experimental.pallas import tpu as pltpu

D_MODEL = 1024
HEAD_DIM = 64
N_HEADS_DIL = 8
DIL_PATTERNS = ((128, 1), (512, 4), (2048, 16))
ROPE_DIM_DIL = HEAD_DIM // 4
ROPE_THETA = 500000.0
ATTN_BLOCK = 128
N_HEADS_MLA = 8
Q_LORA = 384
KV_LORA = 128
QK_NOPE = 64
QK_ROPE = 32
V_HEAD = 64
DIL_WIDTH = N_HEADS_DIL * HEAD_DIM
MLA_WIDTH = N_HEADS_MLA * V_HEAD
N_EXPERTS = 256
TOP_K = 8
N_GROUP = 8
TOPK_GROUP = 4
GROUP_SIZE = N_EXPERTS // N_GROUP
EXPERT_FF = 256
SHARED_FF = 256
ROUTED_SCALE = 2.5
DEPTH = 1
DEEPNORM_ALPHA = (2.0 * DEPTH) ** 0.25
LN_EPS = 1e-5
RMS_EPS = 1e-6
NEG = -1e30
LOG2E = 1.4426950408889634

LANES = 128
SUBLANES = 8
VMEM_LIMIT = 56 * 1024 * 1024

PROJ_TM = 512
DIL_TQ = 2048
MLA_TQ = 512
MLA_TK = 512
POST_TM = 512
ROUTE_TT = 512
DISPATCH_TT = 512
MOE_BM = 256
COMBINE_TT = 128
ROW_LOOP_UNROLL = 4

MLA_HEAD_PAD = LANES

bf16 = jnp.bfloat16
f32 = jnp.float32


def _dot(a, b):
    return jnp.dot(a, b, preferred_element_type=f32)


def _dot_nt(a, b):
    return lax.dot_general(a, b, (((1,), (1,)), ((), ())), preferred_element_type=f32)


def _rope_lanes(x, tab_ref, shift):
    return (x * tab_ref[0]
            + pltpu.roll(x, shift, 1) * tab_ref[1]
            + pltpu.roll(x, LANES - shift, 1) * tab_ref[2])


def _rms(x, g):
    return x * lax.rsqrt(jnp.mean(x * x, axis=-1, keepdims=True) + RMS_EPS) * g


def _layernorm(x, g, b):
    mu = jnp.mean(x, axis=-1, keepdims=True)
    xc = x - mu
    var = jnp.mean(xc * xc, axis=-1, keepdims=True)
    return xc * lax.rsqrt(var + LN_EPS) * g + b


def _silu(x):
    return x * (1.0 / (1.0 + jnp.exp(-x)))


def _proj_kernel(x_ref, win_ref, wuq_ref, wukv_ref, gq_ref, gkv_ref, ropd_ref, ropm_ref,
                 qd_ref, kd_ref, vd_ref, qm_ref, km_ref, vm_ref):
    xb = x_ref[...].astype(bf16)
    o_k, o_v, o_cq, o_ckv, o_kpe = DIL_WIDTH, 2 * DIL_WIDTH, 3 * DIL_WIDTH, 3 * DIL_WIDTH + Q_LORA, \
        3 * DIL_WIDTH + Q_LORA + KV_LORA
    rope_shift_d = ROPE_DIM_DIL // 2
    rope_shift_m = QK_ROPE // 2

    q = _dot(xb, win_ref[:, 0:o_k])
    k = _dot(xb, win_ref[:, o_k:o_v])
    for j in range(DIL_WIDTH // LANES):
        sl = slice(j * LANES, (j + 1) * LANES)
        qd_ref[:, sl] = _rope_lanes(q[:, sl], ropd_ref, rope_shift_d) * (HEAD_DIM ** -0.5 * LOG2E)
        kd_ref[:, sl] = _rope_lanes(k[:, sl], ropd_ref, rope_shift_d)
    vd_ref[...] = _dot(xb, win_ref[:, o_v:o_cq])

    cq = _rms(_dot(xb, win_ref[:, o_cq:o_ckv]), gq_ref[...]).astype(bf16)
    qm = _dot(cq, wuq_ref[...])
    ckv = _rms(_dot(xb, win_ref[:, o_ckv:o_kpe]), gkv_ref[...]).astype(bf16)
    kn = _dot(ckv, wukv_ref[:, 0:N_HEADS_MLA * MLA_HEAD_PAD])
    kpe = _rope_lanes(_dot(xb, win_ref[:, o_kpe:o_kpe + LANES]), ropm_ref, rope_shift_m)
    vv = _dot(ckv, wukv_ref[:, N_HEADS_MLA * MLA_HEAD_PAD:])
    scale = (QK_NOPE + QK_ROPE) ** -0.5 * LOG2E
    ones_pad = (lax.broadcasted_iota(jnp.int32, (1, LANES), 1) >= V_HEAD).astype(f32)
    for h in range(N_HEADS_MLA):
        sl = slice(h * LANES, (h + 1) * LANES)
        qm_ref[:, sl] = (_rope_lanes(qm[:, sl], ropm_ref, rope_shift_m) * scale).astype(bf16)
        km_ref[:, sl] = (kn[:, sl] + kpe).astype(bf16)
        vm_ref[:, sl] = (vv[:, sl] + ones_pad).astype(bf16)


def _rope_tables(seq):
    pos = jnp.arange(seq, dtype=f32)[:, None]
    lane = jnp.arange(LANES)

    def build(dim, lane_in_head, period):
        half = dim // 2
        inv = ROPE_THETA ** (-jnp.arange(0, dim, 2, dtype=f32) / dim)
        ang = pos * inv[None, :]
        cos, sin = jnp.cos(ang), jnp.sin(ang)
        off = lane_in_head(lane % period)
        in_x1 = (off >= 0) & (off < half)
        in_x2 = (off >= half) & (off < dim)
        idx = jnp.clip(jnp.where(in_x2, off - half, off), 0, half - 1)
        c = jnp.where((in_x1 | in_x2)[None, :], cos[:, idx], 1.0)
        sa = jnp.where(in_x2[None, :], sin[:, idx], 0.0)
        sb = jnp.where(in_x1[None, :], -sin[:, idx], 0.0)
        return jnp.stack([c, sa, sb]).astype(f32)

    tab_d = build(ROPE_DIM_DIL, lambda l: jnp.where(l < ROPE_DIM_DIL, l, -1), HEAD_DIM)
    tab_m = build(QK_ROPE, lambda l: jnp.where((l >= QK_NOPE) & (l < QK_NOPE + QK_ROPE), l - QK_NOPE, -1), LANES)
    return tab_d, tab_m


def _project(x2d, w_in, w_uq, w_ukv, g_q, g_kv, seq):
    T = x2d.shape[0]
    tm = PROJ_TM
    n_seq_tiles = seq // tm
    tab_d, tab_m = _rope_tables(seq)

    kpe_cols = jnp.zeros((D_MODEL, LANES), f32).at[:, QK_NOPE:QK_NOPE + QK_ROPE].set(w_in[:, -QK_ROPE:])
    w_in_b = jnp.concatenate([w_in[:, :-QK_ROPE], kpe_cols], axis=1).astype(bf16)
    w_uq_b = jnp.pad(w_uq.reshape(Q_LORA, N_HEADS_MLA, QK_NOPE + QK_ROPE),
                     ((0, 0), (0, 0), (0, MLA_HEAD_PAD - QK_NOPE - QK_ROPE))
                     ).reshape(Q_LORA, N_HEADS_MLA * MLA_HEAD_PAD).astype(bf16)
    w_ukv3 = w_ukv.reshape(KV_LORA, N_HEADS_MLA, QK_NOPE + V_HEAD)
    w_uk = jnp.pad(w_ukv3[:, :, :QK_NOPE], ((0, 0), (0, 0), (0, MLA_HEAD_PAD - QK_NOPE))
                   ).reshape(KV_LORA, N_HEADS_MLA * MLA_HEAD_PAD)
    w_uv = jnp.pad(w_ukv3[:, :, QK_NOPE:], ((0, 0), (0, 0), (0, MLA_HEAD_PAD - V_HEAD))
                   ).reshape(KV_LORA, N_HEADS_MLA * MLA_HEAD_PAD)
    w_ukv_b = jnp.concatenate([w_uk, w_uv], axis=1).astype(bf16)

    full = lambda a: pl.BlockSpec(a.shape, lambda i: (0,) * a.ndim)
    rows = lambda w: pl.BlockSpec((tm, w), lambda i: (i, 0))
    tab_spec = pl.BlockSpec((3, tm, LANES), lambda i: (0, i % n_seq_tiles, 0))
    gq2, gkv2 = g_q.reshape(1, Q_LORA), g_kv.reshape(1, KV_LORA)
    mla_w = N_HEADS_MLA * MLA_HEAD_PAD
    return pl.pallas_call(
        _proj_kernel,
        grid=(T // tm,),
        in_specs=[rows(D_MODEL), full(w_in_b), full(w_uq_b), full(w_ukv_b), full(gq2), full(gkv2),
                  tab_spec, tab_spec],
        out_specs=[rows(DIL_WIDTH), rows(DIL_WIDTH), rows(DIL_WIDTH), rows(mla_w), rows(mla_w), rows(mla_w)],
        out_shape=[jax.ShapeDtypeStruct((T, DIL_WIDTH), f32)] * 3 + [jax.ShapeDtypeStruct((T, mla_w), bf16)] * 3,
        compiler_params=pltpu.CompilerParams(dimension_semantics=("parallel",), vmem_limit_bytes=VMEM_LIMIT),
        name="proj",
    )(x2d, w_in_b, w_uq_b, w_ukv_b, gq2, gkv2, tab_d, tab_m)


def _dilated_kernel(q_ref, kc_ref, kp_ref, vc_ref, vp_ref, o_ref, pv_ref, m_ref):
    blk = ATTN_BLOCK
    has_prev_tile = pl.program_id(2) > 0
    lane = lax.broadcasted_iota(jnp.int32, (1, LANES), 1)
    head_lanes = (lane < HEAD_DIM, lane >= HEAD_DIM)
    qi = lax.broadcasted_iota(jnp.int32, (blk, 2 * blk), 0)
    ki = lax.broadcasted_iota(jnp.int32, (blk, 2 * blk), 1)
    band = (ki >= qi) & (ki <= qi + blk)
    band_seq_start = band & ((ki >= blk) | has_prev_tile)

    for p_idx, (window, d) in enumerate(DIL_PATTERNS):
        assert window // d == blk and DIL_TQ % (d * blk) == 0
        n_blk = DIL_TQ // (d * blk)
        for r in range(d):
            for b in range(n_blk):
                def strided(bb):
                    return pl.ds(r + d * blk * bb, blk, stride=d) if d > 1 else pl.ds(blk * bb, blk)
                rows = strided(b)
                if b > 0:
                    kp, vp, mask = kc_ref[0, strided(b - 1), :], vc_ref[0, strided(b - 1), :], band
                else:
                    kp, vp = kp_ref[0, strided(n_blk - 1), :], vp_ref[0, strided(n_blk - 1), :]
                    mask = band_seq_start
                q = q_ref[0, rows, :]
                kcat = jnp.concatenate([kp, kc_ref[0, rows, :]], axis=0).astype(bf16)
                vcat = jnp.concatenate([vp, vc_ref[0, rows, :]], axis=0)
                for h in range(2):
                    qh = jnp.where(head_lanes[h], q, 0.0).astype(bf16)
                    s = jnp.where(mask, _dot_nt(qh, kcat), NEG)
                    mb = jnp.max(s, axis=1, keepdims=True)
                    p = jnp.exp2(s - mb).astype(bf16)
                    vh = jnp.where(head_lanes[h], vcat, 1.0).astype(bf16)
                    pv_ref[p_idx, h, rows, :] = _dot(p, vh)
                    m_ref[p_idx, h, rows, :] = jnp.broadcast_to(mb, (blk, LANES))

    n_pat = len(DIL_PATTERNS)
    chunk = 2 * blk

    def merge(c, carry):
        rows = pl.ds(pl.multiple_of(c * chunk, chunk), chunk)
        outs = []
        for h in range(2):
            ms = [m_ref[p, h, rows, :] for p in range(n_pat)]
            m_all = functools.reduce(jnp.maximum, ms)
            tot = sum(jnp.exp2(ms[p] - m_all) * pv_ref[p, h, rows, :] for p in range(n_pat))
            outs.append(tot / pltpu.roll(tot, HEAD_DIM, 1))
        o_ref[0, rows, :] = jnp.where(head_lanes[0], outs[0], outs[1]).astype(o_ref.dtype)
        return carry

    lax.fori_loop(0, DIL_TQ // chunk, merge, 0)


def _dilated_attention(qd, kd, vd, batch, seq):
    tq = DIL_TQ
    q3, k3, v3 = (a.reshape(batch, seq, DIL_WIDTH) for a in (qd, kd, vd))
    cur = pl.BlockSpec((1, tq, LANES), lambda b, hp, i: (b, i, hp))
    prev = pl.BlockSpec((1, tq, LANES), lambda b, hp, i: (b, jnp.maximum(i - 1, 0), hp))
    out = pl.pallas_call(
        _dilated_kernel,
        grid=(batch, DIL_WIDTH // LANES, seq // tq),
        in_specs=[cur, cur, prev, cur, prev],
        out_specs=cur,
        out_shape=jax.ShapeDtypeStruct((batch, seq, DIL_WIDTH), bf16),
        scratch_shapes=[pltpu.VMEM((len(DIL_PATTERNS), 2, tq, LANES), f32)] * 2,
        compiler_params=pltpu.CompilerParams(dimension_semantics=("parallel", "parallel", "arbitrary"),
                                             vmem_limit_bytes=VMEM_LIMIT),
        name="dilated_attn",
    )(q3, k3, k3, v3, v3)
    return out.reshape(batch * seq, DIL_WIDTH)


def _mla_kernel(q_ref, k_ref, v_ref, o_ref, acc_ref, m_ref):
    tq, tk = MLA_TQ, MLA_TK
    i = pl.program_id(2)
    lane = lax.broadcasted_iota(jnp.int32, (1, LANES), 1)
    causal = lax.broadcasted_iota(jnp.int32, (tq, tk), 1) <= lax.broadcasted_iota(jnp.int32, (tq, tk), 0)
    m_ref[...] = jnp.full(m_ref.shape, NEG, f32)
    acc_ref[...] = jnp.zeros(acc_ref.shape, f32)

    def step(j, masked):
        rows = pl.ds(pl.multiple_of(j * tk, tk), tk)
        for h in range(2):
            hs = slice(h * LANES, (h + 1) * LANES)
            s = _dot_nt(q_ref[0, :, hs], k_ref[0, rows, hs])
            if masked:
                s = jnp.where(causal, s, NEG)
            m_old = m_ref[h]
            m_new = jnp.maximum(m_old, jnp.max(s, axis=1, keepdims=True))
            p = jnp.concatenate([jnp.exp2(s[:, c * LANES:(c + 1) * LANES] - m_new) for c in range(tk // LANES)],
                                axis=1).astype(bf16)
            acc_ref[h] = acc_ref[h] * jnp.exp2(m_old - m_new) + _dot(p, v_ref[0, rows, hs])
            m_ref[h] = m_new

    def body(j, carry):
        step(j, False)
        return carry

    lax.fori_loop(0, i, body, 0)
    step(i, True)

    acc0, acc1 = acc_ref[0], acc_ref[1]
    out = jnp.where(lane < V_HEAD, acc0 / pltpu.roll(acc0, V_HEAD, 1), pltpu.roll(acc1, V_HEAD, 1) / acc1)
    o_ref[0] = out.astype(o_ref.dtype)


def _mla_attention(qm, km, vm, batch, seq):
    assert MLA_TQ == MLA_TK
    tq = MLA_TQ
    mla_w = N_HEADS_MLA * MLA_HEAD_PAD
    q3 = qm.reshape(batch, seq, mla_w)
    k3 = km.reshape(batch, seq, mla_w)
    v3 = vm.reshape(batch, seq, mla_w)
    out = pl.pallas_call(
        _mla_kernel,
        grid=(batch, N_HEADS_MLA // 2, seq // tq),
        in_specs=[pl.BlockSpec((1, tq, 2 * LANES), lambda b, hp, i: (b, i, hp)),
                  pl.BlockSpec((1, seq, 2 * LANES), lambda b, hp, i: (b, 0, hp)),
                  pl.BlockSpec((1, seq, 2 * LANES), lambda b, hp, i: (b, 0, hp))],
        out_specs=pl.BlockSpec((1, tq, LANES), lambda b, hp, i: (b, i, hp)),
        out_shape=jax.ShapeDtypeStruct((batch, seq, MLA_WIDTH), bf16),
        scratch_shapes=[pltpu.VMEM((2, tq, LANES), f32), pltpu.VMEM((2, tq, LANES), f32)],
        compiler_params=pltpu.CompilerParams(dimension_semantics=("parallel", "parallel", "arbitrary"),
                                             vmem_limit_bytes=VMEM_LIMIT),
        name="mla_attn",
    )(q3, k3, v3)
    return out.reshape(batch * seq, MLA_WIDTH)


def _post_attn_kernel(x_ref, od_ref, om_ref, gd_ref, gm_ref, wo_ref, g1_ref, b1_ref, rw_ref,
                      ws1_ref, ws3_ref, ws2_ref, base_ref, hrow_ref, logit_ref):
    tm = POST_TM
    yd = _rms(od_ref[...].astype(f32), gd_ref[...]).astype(bf16)
    ym = _rms(om_ref[...].astype(f32), gm_ref[...]).astype(bf16)
    mix = _dot(yd, wo_ref[0:DIL_WIDTH, :]) + _dot(ym, wo_ref[DIL_WIDTH:, :])
    h1 = _layernorm(DEEPNORM_ALPHA * x_ref[...] + mix, g1_ref[...], b1_ref[...])
    for s in range(D_MODEL // LANES):
        hrow_ref[pl.ds(s, tm, stride=SUBLANES), :] = h1[:, s * LANES:(s + 1) * LANES]
    hb = h1.astype(bf16)
    logit_ref[...] = _dot_nt(rw_ref[...], hb)
    act = (_silu(_dot(hb, ws1_ref[...])) * _dot(hb, ws3_ref[...])).astype(bf16)
    base_ref[...] = DEEPNORM_ALPHA * h1 + _dot(act, ws2_ref[...])


def _post_attention(x2d, o_dil, o_mla, g_out_dil, g_out_mla, w_o, ln1_g, ln1_b, router_w, ws1, ws3, ws2):
    T = x2d.shape[0]
    tm = POST_TM
    full = lambda a: pl.BlockSpec(a.shape, lambda i: (0,) * a.ndim)
    rows = lambda w: pl.BlockSpec((tm, w), lambda i: (i, 0))
    args = (x2d, o_dil, o_mla, g_out_dil.reshape(1, -1), g_out_mla.reshape(1, -1), w_o.astype(bf16),
            ln1_g.reshape(1, -1), ln1_b.reshape(1, -1), router_w.astype(bf16),
            ws1.astype(bf16), ws3.astype(bf16), ws2.astype(bf16))
    return pl.pallas_call(
        _post_attn_kernel,
        grid=(T // tm,),
        in_specs=[rows(D_MODEL), rows(DIL_WIDTH), rows(MLA_WIDTH)] + [full(a) for a in args[3:]],
        out_specs=[rows(D_MODEL), pl.BlockSpec((tm * SUBLANES, LANES), lambda i: (i, 0)),
                   pl.BlockSpec((N_EXPERTS, tm), lambda i: (0, i))],
        out_shape=[jax.ShapeDtypeStruct((T, D_MODEL), f32),
                   jax.ShapeDtypeStruct((T * SUBLANES, LANES), f32),
                   jax.ShapeDtypeStruct((N_EXPERTS, T), f32)],
        compiler_params=pltpu.CompilerParams(dimension_semantics=("parallel",), vmem_limit_bytes=VMEM_LIMIT),
        name="post_attn",
    )(*args)


def _first_argmax(vals, row_ids, n_rows):
    mx = jnp.max(vals, axis=0, keepdims=True)
    idx = jnp.min(jnp.where(vals == mx, row_ids, n_rows), axis=0, keepdims=True)
    return mx, idx


def _router_kernel(logit_ref, bias_ref, e_ref, rank_ref, gw_ref, count_ref, carry_ref):
    tt = ROUTE_TT
    step = pl.program_id(0)

    @pl.when(step == 0)
    def _():
        carry_ref[...] = jnp.zeros_like(carry_ref)

    scores = 1.0 / (1.0 + jnp.exp(-logit_ref[...]))
    choice = scores + bias_ref[...]
    row = lax.broadcasted_iota(jnp.int32, (N_EXPERTS, tt), 0)
    grow = lax.broadcasted_iota(jnp.int32, (GROUP_SIZE, tt), 0)
    neg_inf = -jnp.inf

    gscore = []
    for g in range(N_GROUP):
        cg = choice[g * GROUP_SIZE:(g + 1) * GROUP_SIZE, :]
        m1, i1 = _first_argmax(cg, grow, GROUP_SIZE)
        m2 = jnp.max(jnp.where(grow == i1, neg_inf, cg), axis=0, keepdims=True)
        gscore.append(m1 + m2)
    masked = []
    for g in range(N_GROUP):
        beaten = jnp.zeros((1, tt), jnp.int32)
        for o in range(N_GROUP):
            if o == g:
                continue
            wins = (gscore[o] >= gscore[g]) if o < g else (gscore[o] > gscore[g])
            beaten = beaten + wins.astype(jnp.int32)
        keep = beaten < TOPK_GROUP
        masked.append(jnp.where(keep, choice[g * GROUP_SIZE:(g + 1) * GROUP_SIZE, :], neg_inf))
    cur = jnp.concatenate(masked, axis=0)

    sel_idx, sel_score = [], []
    for _ in range(TOP_K):
        _, idx = _first_argmax(cur, row, N_EXPERTS)
        hit = row == idx
        sel_idx.append(idx)
        sel_score.append(jnp.sum(jnp.where(hit, scores, 0.0), axis=0, keepdims=True))
        cur = jnp.where(hit, neg_inf, cur)
    onehot = jnp.zeros((N_EXPERTS, tt), f32)
    for idx in sel_idx:
        onehot = onehot + (row == idx).astype(f32)

    upper = (lax.broadcasted_iota(jnp.int32, (tt, tt), 0) < lax.broadcasted_iota(jnp.int32, (tt, tt), 1))
    before = _dot(onehot.astype(bf16), upper.astype(bf16)) + carry_ref[...]
    carry_ref[...] = carry_ref[...] + jnp.sum(onehot, axis=1, keepdims=True)
    count_ref[...] = carry_ref[...].astype(jnp.int32)

    denom = sel_score[0]
    for s in sel_score[1:]:
        denom = denom + s
    for k in range(TOP_K):
        hit = row == sel_idx[k]
        e_ref[k:k + 1, :] = sel_idx[k]
        rank_ref[k:k + 1, :] = jnp.sum(jnp.where(hit, before, 0.0), axis=0, keepdims=True).astype(jnp.int32)
        gw_ref[k:k + 1, :] = sel_score[k] / denom * ROUTED_SCALE


def _route(logits_t, router_bias):
    T = logits_t.shape[1]
    tt = ROUTE_TT
    slot = pl.BlockSpec((TOP_K, tt), lambda i: (0, i))
    col = pl.BlockSpec((N_EXPERTS, 1), lambda i: (0, 0))
    return pl.pallas_call(
        _router_kernel,
        grid=(T // tt,),
        in_specs=[pl.BlockSpec((N_EXPERTS, tt), lambda i: (0, i)), col],
        out_specs=[slot, slot, slot, col],
        out_shape=[jax.ShapeDtypeStruct((TOP_K, T), jnp.int32), jax.ShapeDtypeStruct((TOP_K, T), jnp.int32),
                   jax.ShapeDtypeStruct((TOP_K, T), f32), jax.ShapeDtypeStruct((N_EXPERTS, 1), jnp.int32)],
        scratch_shapes=[pltpu.VMEM((N_EXPERTS, 1), f32)],
        compiler_params=pltpu.CompilerParams(dimension_semantics=("arbitrary",), vmem_limit_bytes=VMEM_LIMIT),
        name="router",
    )(logits_t, router_bias.reshape(N_EXPERTS, 1))


def _token_tile(ref, idx):
    return ref.at[pl.ds(pl.multiple_of(idx * SUBLANES, SUBLANES), SUBLANES), :]


def _dispatch_kernel(start_ref, e_ref, rank_ref, h_ref, xs_hbm, sem):
    tt = DISPATCH_TT

    def issue(j, carry):
        src = _token_tile(h_ref, j)
        for k in range(TOP_K):
            dest = start_ref[e_ref[k, j]] + rank_ref[k, j]
            pltpu.make_async_copy(src, _token_tile(xs_hbm, dest), sem).start()
        return carry

    lax.fori_loop(0, tt, issue, 0, unroll=ROW_LOOP_UNROLL)
    for _ in range(TOP_K):
        pltpu.make_async_copy(h_ref, xs_hbm.at[pl.ds(0, tt * SUBLANES), :], sem).wait()


def _dispatch(start, e_t, rank_t, h_rows):
    T = e_t.shape[1]
    tt = DISPATCH_TT
    slot = pl.BlockSpec((TOP_K, tt), lambda i, *_: (0, i), memory_space=pltpu.SMEM)
    return pl.pallas_call(
        _dispatch_kernel,
        grid_spec=pltpu.PrefetchScalarGridSpec(
            num_scalar_prefetch=1,
            grid=(T // tt,),
            in_specs=[slot, slot, pl.BlockSpec((tt * SUBLANES, LANES), lambda i, *_: (i, 0))],
            out_specs=pl.BlockSpec(memory_space=pl.ANY),
            scratch_shapes=[pltpu.SemaphoreType.DMA(())],
        ),
        out_shape=jax.ShapeDtypeStruct((T * TOP_K * SUBLANES, LANES), f32),
        compiler_params=pltpu.CompilerParams(dimension_semantics=("arbitrary",)),
        name="dispatch",
    )(start, e_t, rank_t, h_rows)


def _moe_kernel(tile_ref, exp_ref, valid_ref, start_ref, end_ref, x_ref, w1_ref, w3_ref, w2_ref, y_ref,
                xb_ref, w1b_ref, w3b_ref, w2b_ref):
    bm = MOE_BM
    i = pl.program_id(0)
    prev = jnp.maximum(i - 1, 0)
    e, tile = exp_ref[i], tile_ref[i]
    new_expert = (i == 0) | (e != exp_ref[prev])
    new_tile = (i == 0) | (tile != tile_ref[prev])

    @pl.when(new_expert)
    def _():
        w1b_ref[...] = w1_ref[...].astype(bf16)
        w3b_ref[...] = w3_ref[...].astype(bf16)
        w2b_ref[...] = w2_ref[...].astype(bf16)

    @pl.when(new_tile)
    def _():
        for s in range(D_MODEL // LANES):
            xb_ref[:, s * LANES:(s + 1) * LANES] = x_ref[pl.ds(s, bm, stride=SUBLANES), :].astype(bf16)
        y_ref[...] = jnp.zeros_like(y_ref)

    @pl.when(valid_ref[i] == 1)
    def _():
        xb = xb_ref[...]
        act = (_silu(_dot(xb, w1b_ref[...])) * _dot(xb, w3b_ref[...])).astype(bf16)
        y = _dot(act, w2b_ref[...])
        row = tile * bm + lax.broadcasted_iota(jnp.int32, (bm, 1), 0)
        mine = (row >= start_ref[e]) & (row < end_ref[e])
        for s in range(D_MODEL // LANES):
            rows = pl.ds(s, bm, stride=SUBLANES)
            y_ref[rows, :] = jnp.where(mine, y[:, s * LANES:(s + 1) * LANES], y_ref[rows, :])


def _work_items(counts, n_rows):
    bm = MOE_BM
    n_tiles = n_rows // bm
    n_items = n_tiles + N_EXPERTS
    end = jnp.cumsum(counts)
    start = end - counts
    first_tile = start // bm
    last_tile = jnp.maximum(end - 1, 0) // bm
    per_expert = jnp.where(counts > 0, last_tile - first_tile + 1, 0)
    item_end = jnp.cumsum(per_expert)
    item_start = item_end - per_expert
    ids = jnp.arange(n_items, dtype=jnp.int32)
    used = item_end[-1]
    exp_of = jnp.minimum(jnp.searchsorted(item_end, ids, side="right"), N_EXPERTS - 1).astype(jnp.int32)
    tile_of = (first_tile[exp_of] + ids - item_start[exp_of]).astype(jnp.int32)
    valid = ids < used
    last = jnp.maximum(used - 1, 0)
    exp_of = jnp.where(valid, exp_of, exp_of[last])
    tile_of = jnp.where(valid, tile_of, tile_of[last])
    return tile_of, exp_of, valid.astype(jnp.int32), start.astype(jnp.int32), end.astype(jnp.int32)


def _experts(items, xs, w1, w3, w2):
    bm = MOE_BM
    tile_of, exp_of, valid, start, end = items
    n_items = tile_of.shape[0]
    rows = pl.BlockSpec((bm * SUBLANES, LANES), lambda i, t, e, *_: (t[i], 0))
    wspec = lambda a: pl.BlockSpec((None, None) + a.shape[2:], lambda i, t, e, *_: (0, e[i], 0, 0))
    return pl.pallas_call(
        _moe_kernel,
        grid_spec=pltpu.PrefetchScalarGridSpec(
            num_scalar_prefetch=5,
            grid=(n_items,),
            in_specs=[rows, wspec(w1), wspec(w3), wspec(w2)],
            out_specs=rows,
            scratch_shapes=[pltpu.VMEM((bm, D_MODEL), bf16), pltpu.VMEM(w1.shape[2:], bf16),
                            pltpu.VMEM(w3.shape[2:], bf16), pltpu.VMEM(w2.shape[2:], bf16)],
        ),
        out_shape=jax.ShapeDtypeStruct(xs.shape, f32),
        compiler_params=pltpu.CompilerParams(dimension_semantics=("arbitrary",), vmem_limit_bytes=VMEM_LIMIT),
        name="experts",
    )(tile_of, exp_of, valid, start, end, xs, w1, w3, w2)


def _combine_kernel(start_ref, e_ref, rank_ref, gw_ref, e_next_ref, rank_next_ref, y_hbm, base_ref, g2_ref, b2_ref,
                    o_ref, buf_ref, acc_ref, sem):
    tt = COMBINE_TT
    i = pl.program_id(0)
    cur = i % 2

    def gather(e_r, rank_r, slot):
        def issue(j, carry):
            for k in range(TOP_K):
                src = _token_tile(y_hbm, start_ref[e_r[k, j]] + rank_r[k, j])
                pltpu.make_async_copy(src, _token_tile(buf_ref.at[slot, k], j), sem.at[slot]).start()
            return carry

        lax.fori_loop(0, tt, issue, 0, unroll=ROW_LOOP_UNROLL)

    @pl.when(i == 0)
    def _():
        gather(e_ref, rank_ref, 0)

    @pl.when(i + 1 < pl.num_programs(0))
    def _():
        gather(e_next_ref, rank_next_ref, 1 - cur)

    for k in range(TOP_K):
        pltpu.make_async_copy(y_hbm.at[pl.ds(0, tt * SUBLANES), :], buf_ref.at[cur, k], sem.at[cur]).wait()

    def weigh(j, carry):
        rows = pl.ds(pl.multiple_of(j * SUBLANES, SUBLANES), SUBLANES)
        tot = gw_ref[0, j] * buf_ref[cur, 0, rows, :]
        for k in range(1, TOP_K):
            tot = tot + gw_ref[k, j] * buf_ref[cur, k, rows, :]
        acc_ref[rows, :] = tot
        return carry

    lax.fori_loop(0, tt, weigh, 0, unroll=ROW_LOOP_UNROLL)
    routed = jnp.concatenate([acc_ref[pl.ds(s, tt, stride=SUBLANES), :] for s in range(D_MODEL // LANES)], axis=1)
    o_ref[...] = _layernorm(base_ref[...] + routed, g2_ref[...], b2_ref[...])


def _combine(start, e_t, rank_t, gw_t, y_rows, base, ln2_g, ln2_b):
    T = base.shape[0]
    tt = COMBINE_TT
    n_steps = T // tt
    slot = pl.BlockSpec((TOP_K, tt), lambda i, *_: (0, i), memory_space=pltpu.SMEM)
    slot_next = pl.BlockSpec((TOP_K, tt), lambda i, *_: (0, jnp.minimum(i + 1, n_steps - 1)),
                             memory_space=pltpu.SMEM)
    vec = pl.BlockSpec((1, D_MODEL), lambda i, *_: (0, 0))
    rows = pl.BlockSpec((tt, D_MODEL), lambda i, *_: (i, 0))
    return pl.pallas_call(
        _combine_kernel,
        grid_spec=pltpu.PrefetchScalarGridSpec(
            num_scalar_prefetch=1,
            grid=(n_steps,),
            in_specs=[slot, slot, slot, slot_next, slot_next, pl.BlockSpec(memory_space=pl.ANY), rows, vec, vec],
            out_specs=rows,
            scratch_shapes=[pltpu.VMEM((2, TOP_K, tt * SUBLANES, LANES), f32),
                            pltpu.VMEM((tt * SUBLANES, LANES), f32), pltpu.SemaphoreType.DMA((2,))],
        ),
        out_shape=jax.ShapeDtypeStruct((T, D_MODEL), f32),
        compiler_params=pltpu.CompilerParams(dimension_semantics=("arbitrary",), vmem_limit_bytes=VMEM_LIMIT),
        name="combine",
    )(start, e_t, rank_t, gw_t, e_t, rank_t, y_rows, base, ln2_g.reshape(1, -1), ln2_b.reshape(1, -1))


def _layer(x, w_in, g_q, w_uq, g_kv, w_ukv, g_out_dil, g_out_mla, w_o, ln1_g, ln1_b,
           router_w, router_bias, w1, w3, w2, ws1, ws3, ws2, ln2_g, ln2_b, layer):
    batch, seq, _ = x.shape
    T = batch * seq
    x2d = x.reshape(T, D_MODEL)
    qd, kd, vd, qm, km, vm = _project(x2d, w_in[layer], w_uq[layer], w_ukv[layer], g_q[layer], g_kv[layer], seq)
    o_dil = _dilated_attention(qd, kd, vd, batch, seq)
    o_mla = _mla_attention(qm, km, vm, batch, seq)
    base, h_rows, logits_t = _post_attention(
        x2d, o_dil, o_mla, g_out_dil[layer], g_out_mla[layer], w_o[layer], ln1_g[layer], ln1_b[layer],
        router_w[layer], ws1[layer], ws3[layer], ws2[layer])
    e_t, rank_t, gw_t, counts = _route(logits_t, router_bias[layer])
    items = _work_items(counts.reshape(N_EXPERTS), T * TOP_K)
    start = items[3]
    xs = _dispatch(start, e_t, rank_t, h_rows)
    y_rows = _experts(items, xs, w1[layer:layer + 1], w3[layer:layer + 1], w2[layer:layer + 1])
    out = _combine(start, e_t, rank_t, gw_t, y_rows, base, ln2_g[layer], ln2_b[layer])
    return out.reshape(batch, seq, D_MODEL)


def kernel(x, w_in, g_q, w_uq, g_kv, w_ukv, g_out_dil, g_out_mla, w_o, ln1_g, ln1_b, router_w, router_bias,
           w1, w3, w2, ws1, ws3, ws2, ln2_g, ln2_b):
    assert x.shape[1] % DIL_TQ == 0 and x.shape[2] == D_MODEL
    h = x
    for layer in range(w_in.shape[0]):
        h = _layer(h, w_in, g_q, w_uq, g_kv, w_ukv, g_out_dil, g_out_mla, w_o, ln1_g, ln1_b,
                   router_w, router_bias, w1, w3, w2, ws1, ws3, ws2, ln2_g, ln2_b, layer)
    return h
```

```python
import functools

import jax
import jax.numpy as jnp
from jax import lax
from jax.experimental import pallas as pl
from jax.experimental.pallas import tpu as pltpu

D_MODEL = 1024
HEAD_DIM = 64
N_HEADS_DIL = 8
DIL_PATTERNS = ((128, 1), (512, 4), (2048, 16))
ROPE_DIM_DIL = HEAD_DIM // 4
ROPE_THETA = 500000.0
ATTN_BLOCK = 128
N_HEADS_MLA = 8
Q_LORA = 384
KV_LORA = 128
QK_NOPE = 64
QK_ROPE = 32
V_HEAD = 64
DIL_WIDTH = N_HEADS_DIL * HEAD_DIM
MLA_WIDTH = N_HEADS_MLA * V_HEAD
N_EXPERTS = 256
TOP_K = 8
N_GROUP = 8
TOPK_GROUP = 4
GROUP_SIZE = N_EXPERTS // N_GROUP
EXPERT_FF = 256
SHARED_FF = 256
ROUTED_SCALE = 2.5
DEPTH = 1
DEEPNORM_ALPHA = (2.0 * DEPTH) ** 0.25
LN_EPS = 1e-5
RMS_EPS = 1e-6
NEG = -1e30
LOG2E = 1.4426950408889634

LANES = 128
SUBLANES = 8
VMEM_LIMIT = 56 * 1024 * 1024

PROJ_TM = 512
DIL_TQ = 2048
MLA_TQ = 512
MLA_TK = 512
POST_TM = 512
ROUTE_TT = 512
DISPATCH_TT = 512
MOE_BM = 512
COMBINE_TT = 256
ROW_LOOP_UNROLL = 4

MLA_HEAD_PAD = LANES

bf16 = jnp.bfloat16
f32 = jnp.float32


def _dot(a, b):
    return jnp.dot(a, b, preferred_element_type=f32)


def _dot_nt(a, b):
    return lax.dot_general(a, b, (((1,), (1,)), ((), ())), preferred_element_type=f32)


def _rope_lanes(x, tab_ref, shift):
    return (x * tab_ref[0]
            + pltpu.roll(x, shift, 1) * tab_ref[1]
            + pltpu.roll(x, LANES - shift, 1) * tab_ref[2])


def _rms(x, g):
    return x * lax.rsqrt(jnp.mean(x * x, axis=-1, keepdims=True) + RMS_EPS) * g


def _layernorm(x, g, b):
    mu = jnp.mean(x, axis=-1, keepdims=True)
    xc = x - mu
    var = jnp.mean(xc * xc, axis=-1, keepdims=True)
    return xc * lax.rsqrt(var + LN_EPS) * g + b


def _silu(x):
    return x * (1.0 / (1.0 + jnp.exp(-x)))


ROW_TILE = D_MODEL // LANES
assert ROW_TILE == SUBLANES


def _proj_kernel(x_ref, win_ref, wuq_ref, wukv_ref, gq_ref, gkv_ref, ropd_ref, ropm_ref,
                 qd_ref, kd_ref, vd_ref, qm_ref, km_ref, vm_ref):
    xb = x_ref[...].astype(bf16)
    o_k, o_v, o_cq, o_ckv, o_kpe = DIL_WIDTH, 2 * DIL_WIDTH, 3 * DIL_WIDTH, 3 * DIL_WIDTH + Q_LORA, \
        3 * DIL_WIDTH + Q_LORA + KV_LORA
    rope_shift_d = ROPE_DIM_DIL // 2
    rope_shift_m = QK_ROPE // 2

    q = _dot(xb, win_ref[:, 0:o_k])
    k = _dot(xb, win_ref[:, o_k:o_v])
    for j in range(DIL_WIDTH // LANES):
        sl = slice(j * LANES, (j + 1) * LANES)
        qd_ref[:, sl] = _rope_lanes(q[:, sl], ropd_ref, rope_shift_d) * (HEAD_DIM ** -0.5 * LOG2E)
        kd_ref[:, sl] = _rope_lanes(k[:, sl], ropd_ref, rope_shift_d)
    vd_ref[...] = _dot(xb, win_ref[:, o_v:o_cq])

    cq = _rms(_dot(xb, win_ref[:, o_cq:o_ckv]), gq_ref[...]).astype(bf16)
    qm = _dot(cq, wuq_ref[...])
    ckv = _rms(_dot(xb, win_ref[:, o_ckv:o_kpe]), gkv_ref[...]).astype(bf16)
    kn = _dot(ckv, wukv_ref[:, 0:N_HEADS_MLA * MLA_HEAD_PAD])
    kpe = _rope_lanes(_dot(xb, win_ref[:, o_kpe:o_kpe + LANES]), ropm_ref, rope_shift_m)
    vv = _dot(ckv, wukv_ref[:, N_HEADS_MLA * MLA_HEAD_PAD:])
    scale = (QK_NOPE + QK_ROPE) ** -0.5 * LOG2E
    ones_pad = (lax.broadcasted_iota(jnp.int32, (1, LANES), 1) >= V_HEAD).astype(f32)
    for h in range(N_HEADS_MLA):
        sl = slice(h * LANES, (h + 1) * LANES)
        qm_ref[:, sl] = (_rope_lanes(qm[:, sl], ropm_ref, rope_shift_m) * scale).astype(bf16)
        km_ref[:, sl] = (kn[:, sl] + kpe).astype(bf16)
        vm_ref[:, sl] = (vv[:, sl] + ones_pad).astype(bf16)


def _rope_tables(seq):
    pos = jnp.arange(seq, dtype=f32)[:, None]
    lane = jnp.arange(LANES)

    def build(dim, lane_in_head, period):
        half = dim // 2
        inv = ROPE_THETA ** (-jnp.arange(0, dim, 2, dtype=f32) / dim)
        ang = pos * inv[None, :]
        cos, sin = jnp.cos(ang), jnp.sin(ang)
        off = lane_in_head(lane % period)
        in_x1 = (off >= 0) & (off < half)
        in_x2 = (off >= half) & (off < dim)
        idx = jnp.clip(jnp.where(in_x2, off - half, off), 0, half - 1)
        c = jnp.where((in_x1 | in_x2)[None, :], cos[:, idx], 1.0)
        sa = jnp.where(in_x2[None, :], sin[:, idx], 0.0)
        sb = jnp.where(in_x1[None, :], -sin[:, idx], 0.0)
        return jnp.stack([c, sa, sb]).astype(f32)

    tab_d = build(ROPE_DIM_DIL, lambda l: jnp.where(l < ROPE_DIM_DIL, l, -1), HEAD_DIM)
    tab_m = build(QK_ROPE, lambda l: jnp.where((l >= QK_NOPE) & (l < QK_NOPE + QK_ROPE), l - QK_NOPE, -1), LANES)
    return tab_d, tab_m


def _project(x2d, w_in, w_uq, w_ukv, g_q, g_kv, seq):
    T = x2d.shape[0]
    tm = PROJ_TM
    n_seq_tiles = seq // tm
    tab_d, tab_m = _rope_tables(seq)

    kpe_cols = jnp.zeros((D_MODEL, LANES), f32).at[:, QK_NOPE:QK_NOPE + QK_ROPE].set(w_in[:, -QK_ROPE:])
    w_in_b = jnp.concatenate([w_in[:, :-QK_ROPE], kpe_cols], axis=1).astype(bf16)
    w_uq_b = jnp.pad(w_uq.reshape(Q_LORA, N_HEADS_MLA, QK_NOPE + QK_ROPE),
                     ((0, 0), (0, 0), (0, MLA_HEAD_PAD - QK_NOPE - QK_ROPE))
                     ).reshape(Q_LORA, N_HEADS_MLA * MLA_HEAD_PAD).astype(bf16)
    w_ukv3 = w_ukv.reshape(KV_LORA, N_HEADS_MLA, QK_NOPE + V_HEAD)
    w_uk = jnp.pad(w_ukv3[:, :, :QK_NOPE], ((0, 0), (0, 0), (0, MLA_HEAD_PAD - QK_NOPE))
                   ).reshape(KV_LORA, N_HEADS_MLA * MLA_HEAD_PAD)
    w_uv = jnp.pad(w_ukv3[:, :, QK_NOPE:], ((0, 0), (0, 0), (0, MLA_HEAD_PAD - V_HEAD))
                   ).reshape(KV_LORA, N_HEADS_MLA * MLA_HEAD_PAD)
    w_ukv_b = jnp.concatenate([w_uk, w_uv], axis=1).astype(bf16)

    full = lambda a: pl.BlockSpec(a.shape, lambda i: (0,) * a.ndim)
    rows = lambda w: pl.BlockSpec((tm, w), lambda i: (i, 0))
    tab_spec = pl.BlockSpec((3, tm, LANES), lambda i: (0, i % n_seq_tiles, 0))
    gq2, gkv2 = g_q.reshape(1, Q_LORA), g_kv.reshape(1, KV_LORA)
    mla_w = N_HEADS_MLA * MLA_HEAD_PAD
    return pl.pallas_call(
        _proj_kernel,
        grid=(T // tm,),
        in_specs=[rows(D_MODEL), full(w_in_b), full(w_uq_b), full(w_ukv_b), full(gq2), full(gkv2),
                  tab_spec, tab_spec],
        out_specs=[rows(DIL_WIDTH), rows(DIL_WIDTH), rows(DIL_WIDTH), rows(mla_w), rows(mla_w), rows(mla_w)],
        out_shape=[jax.ShapeDtypeStruct((T, DIL_WIDTH), f32)] * 3 + [jax.ShapeDtypeStruct((T, mla_w), bf16)] * 3,
        compiler_params=pltpu.CompilerParams(dimension_semantics=("parallel",), vmem_limit_bytes=VMEM_LIMIT),
        name="proj",
    )(x2d, w_in_b, w_uq_b, w_ukv_b, gq2, gkv2, tab_d, tab_m)


def _dilated_kernel(q_ref, kc_ref, kp_ref, vc_ref, vp_ref, o_ref, pv_ref, m_ref):
    blk = ATTN_BLOCK
    has_prev_tile = pl.program_id(2) > 0
    lane = lax.broadcasted_iota(jnp.int32, (1, LANES), 1)
    head_lanes = (lane < HEAD_DIM, lane >= HEAD_DIM)
    qi = lax.broadcasted_iota(jnp.int32, (blk, 2 * blk), 0)
    ki = lax.broadcasted_iota(jnp.int32, (blk, 2 * blk), 1)
    band = (ki >= qi) & (ki <= qi + blk)
    band_seq_start = band & ((ki >= blk) | has_prev_tile)

    for p_idx, (window, d) in enumerate(DIL_PATTERNS):
        assert window // d == blk and DIL_TQ % (d * blk) == 0
        n_blk = DIL_TQ // (d * blk)
        for r in range(d):
            for b in range(n_blk):
                def strided(bb):
                    return pl.ds(r + d * blk * bb, blk, stride=d) if d > 1 else pl.ds(blk * bb, blk)
                rows = strided(b)
                if b > 0:
                    kp, vp, mask = kc_ref[0, strided(b - 1), :], vc_ref[0, strided(b - 1), :], band
                else:
                    kp, vp = kp_ref[0, strided(n_blk - 1), :], vp_ref[0, strided(n_blk - 1), :]
                    mask = band_seq_start
                q = q_ref[0, rows, :]
                kcat = jnp.concatenate([kp, kc_ref[0, rows, :]], axis=0).astype(bf16)
                vcat = jnp.concatenate([vp, vc_ref[0, rows, :]], axis=0)
                for h in range(2):
                    qh = jnp.where(head_lanes[h], q, 0.0).astype(bf16)
                    s = jnp.where(mask, _dot_nt(qh, kcat), NEG)
                    mb = jnp.max(s, axis=1, keepdims=True)
                    p = jnp.exp2(s - mb).astype(bf16)
                    vh = jnp.where(head_lanes[h], vcat, 1.0).astype(bf16)
                    pv_ref[p_idx, h, rows, :] = _dot(p, vh)
                    m_ref[p_idx, h, rows, :] = jnp.broadcast_to(mb, (blk, LANES))

    n_pat = len(DIL_PATTERNS)
    chunk = 2 * blk

    def merge(c, carry):
        rows = pl.ds(pl.multiple_of(c * chunk, chunk), chunk)
        outs = []
        for h in range(2):
            ms = [m_ref[p, h, rows, :] for p in range(n_pat)]
            m_all = functools.reduce(jnp.maximum, ms)
            tot = sum(jnp.exp2(ms[p] - m_all) * pv_ref[p, h, rows, :] for p in range(n_pat))
            outs.append(tot / pltpu.roll(tot, HEAD_DIM, 1))
        o_ref[0, rows, :] = jnp.where(head_lanes[0], outs[0], outs[1]).astype(o_ref.dtype)
        return carry

    lax.fori_loop(0, DIL_TQ // chunk, merge, 0)


def _dilated_attention(qd, kd, vd, batch, seq):
    tq = DIL_TQ
    q3, k3, v3 = (a.reshape(batch, seq, DIL_WIDTH) for a in (qd, kd, vd))
    cur = pl.BlockSpec((1, tq, LANES), lambda b, hp, i: (b, i, hp))
    prev = pl.BlockSpec((1, tq, LANES), lambda b, hp, i: (b, jnp.maximum(i - 1, 0), hp))
    out = pl.pallas_call(
        _dilated_kernel,
        grid=(batch, DIL_WIDTH // LANES, seq // tq),
        in_specs=[cur, cur, prev, cur, prev],
        out_specs=cur,
        out_shape=jax.ShapeDtypeStruct((batch, seq, DIL_WIDTH), bf16),
        scratch_shapes=[pltpu.VMEM((len(DIL_PATTERNS), 2, tq, LANES), f32)] * 2,
        compiler_params=pltpu.CompilerParams(dimension_semantics=("parallel", "parallel", "arbitrary"),
                                             vmem_limit_bytes=VMEM_LIMIT),
        name="dilated_attn",
    )(q3, k3, k3, v3, v3)
    return out.reshape(batch * seq, DIL_WIDTH)


def _mla_kernel(q_ref, k_ref, v_ref, o_ref, acc_ref, m_ref):
    tq, tk = MLA_TQ, MLA_TK
    i = pl.program_id(2)
    lane = lax.broadcasted_iota(jnp.int32, (1, LANES), 1)
    m_ref[...] = jnp.full(m_ref.shape, NEG, f32)
    acc_ref[...] = jnp.zeros(acc_ref.shape, f32)

    def step(j, n_tiles, masked):
        nk = n_tiles * tk
        rows = pl.ds(pl.multiple_of(j * tk, tk), nk)
        if masked:
            col = lax.broadcasted_iota(jnp.int32, (tq, nk), 1)
            causal = col <= lax.broadcasted_iota(jnp.int32, (tq, nk), 0) + (nk - tk)
        for h in range(2):
            hs = slice(h * LANES, (h + 1) * LANES)
            s = _dot_nt(q_ref[0, :, hs], k_ref[0, rows, hs])
            if masked:
                s = jnp.where(causal, s, NEG)
            m_old = m_ref[h]
            m_new = jnp.maximum(m_old, jnp.max(s, axis=1, keepdims=True))
            p = jnp.concatenate([jnp.exp2(s[:, c * LANES:(c + 1) * LANES] - m_new) for c in range(nk // LANES)],
                                axis=1).astype(bf16)
            acc_ref[h] = acc_ref[h] * jnp.exp2(m_old - m_new) + _dot(p, v_ref[0, rows, hs])
            m_ref[h] = m_new

    def body(j2, carry):
        step(2 * j2, 2, False)
        return carry

    lax.fori_loop(0, i // 2, body, 0)

    @pl.when(i % 2 == 0)
    def _():
        step(i, 1, True)

    @pl.when(i % 2 == 1)
    def _():
        step(i - 1, 2, True)

    acc0, acc1 = acc_ref[0], acc_ref[1]
    out = jnp.where(lane < V_HEAD, acc0 / pltpu.roll(acc0, V_HEAD, 1), pltpu.roll(acc1, V_HEAD, 1) / acc1)
    o_ref[0] = out.astype(o_ref.dtype)


def _mla_attention(qm, km, vm, batch, seq):
    assert MLA_TQ == MLA_TK
    tq = MLA_TQ
    mla_w = N_HEADS_MLA * MLA_HEAD_PAD
    q3 = qm.reshape(batch, seq, mla_w)
    k3 = km.reshape(batch, seq, mla_w)
    v3 = vm.reshape(batch, seq, mla_w)
    out = pl.pallas_call(
        _mla_kernel,
        grid=(batch, N_HEADS_MLA // 2, seq // tq),
        in_specs=[pl.BlockSpec((1, tq, 2 * LANES), lambda b, hp, i: (b, i, hp)),
                  pl.BlockSpec((1, seq, 2 * LANES), lambda b, hp, i: (b, 0, hp)),
                  pl.BlockSpec((1, seq, 2 * LANES), lambda b, hp, i: (b, 0, hp))],
        out_specs=pl.BlockSpec((1, tq, LANES), lambda b, hp, i: (b, i, hp)),
        out_shape=jax.ShapeDtypeStruct((batch, seq, MLA_WIDTH), bf16),
        scratch_shapes=[pltpu.VMEM((2, tq, LANES), f32), pltpu.VMEM((2, tq, LANES), f32)],
        compiler_params=pltpu.CompilerParams(dimension_semantics=("parallel", "parallel", "arbitrary"),
                                             vmem_limit_bytes=VMEM_LIMIT),
        name="mla_attn",
    )(q3, k3, v3)
    return out.reshape(batch * seq, MLA_WIDTH)


def _post_attn_kernel(x_ref, od_ref, om_ref, gd_ref, gm_ref, wo_ref, g1_ref, b1_ref, rw_ref,
                      ws1_ref, ws3_ref, ws2_ref, base_ref, hrow_ref, logit_ref):
    tm = POST_TM
    yd = _rms(od_ref[...].astype(f32), gd_ref[...]).astype(bf16)
    ym = _rms(om_ref[...].astype(f32), gm_ref[...]).astype(bf16)
    mix = _dot(yd, wo_ref[0:DIL_WIDTH, :]) + _dot(ym, wo_ref[DIL_WIDTH:, :])
    h1 = _layernorm(DEEPNORM_ALPHA * x_ref[...] + mix, g1_ref[...], b1_ref[...])
    for s in range(ROW_TILE):
        hrow_ref[pl.ds(s, tm, stride=ROW_TILE), :] = h1[:, s * LANES:(s + 1) * LANES]
    hb = h1.astype(bf16)
    logit_ref[...] = _dot_nt(rw_ref[...], hb)
    act = (_silu(_dot(hb, ws1_ref[...])) * _dot(hb, ws3_ref[...])).astype(bf16)
    base_ref[...] = DEEPNORM_ALPHA * h1 + _dot(act, ws2_ref[...])


def _post_attention(x2d, o_dil, o_mla, g_out_dil, g_out_mla, w_o, ln1_g, ln1_b, router_w, ws1, ws3, ws2):
    T = x2d.shape[0]
    tm = POST_TM
    full = lambda a: pl.BlockSpec(a.shape, lambda i: (0,) * a.ndim)
    rows = lambda w: pl.BlockSpec((tm, w), lambda i: (i, 0))
    args = (x2d, o_dil, o_mla, g_out_dil.reshape(1, -1), g_out_mla.reshape(1, -1), w_o.astype(bf16),
            ln1_g.reshape(1, -1), ln1_b.reshape(1, -1), router_w.astype(bf16),
            ws1.astype(bf16), ws3.astype(bf16), ws2.astype(bf16))
    return pl.pallas_call(
        _post_attn_kernel,
        grid=(T // tm,),
        in_specs=[rows(D_MODEL), rows(DIL_WIDTH), rows(MLA_WIDTH)] + [full(a) for a in args[3:]],
        out_specs=[rows(D_MODEL), pl.BlockSpec((tm * ROW_TILE, LANES), lambda i: (i, 0)),
                   pl.BlockSpec((N_EXPERTS, tm), lambda i: (0, i))],
        out_shape=[jax.ShapeDtypeStruct((T, D_MODEL), f32),
                   jax.ShapeDtypeStruct((T * ROW_TILE, LANES), f32),
                   jax.ShapeDtypeStruct((N_EXPERTS, T), f32)],
        compiler_params=pltpu.CompilerParams(dimension_semantics=("parallel",), vmem_limit_bytes=VMEM_LIMIT),
        name="post_attn",
    )(*args)


def _first_argmax(vals, row_ids, n_rows):
    mx = jnp.max(vals, axis=0, keepdims=True)
    idx = jnp.min(jnp.where(vals == mx, row_ids, n_rows), axis=0, keepdims=True)
    return mx, idx


def _router_kernel(logit_ref, bias_ref, e_ref, rank_ref, gw_ref, count_ref, carry_ref):
    tt = ROUTE_TT
    step = pl.program_id(0)

    @pl.when(step == 0)
    def _():
        carry_ref[...] = jnp.zeros_like(carry_ref)

    scores = 1.0 / (1.0 + jnp.exp(-logit_ref[...]))
    choice = scores + bias_ref[...]
    row = lax.broadcasted_iota(jnp.int32, (N_EXPERTS, tt), 0)
    grow = lax.broadcasted_iota(jnp.int32, (GROUP_SIZE, tt), 0)
    neg_inf = -jnp.inf

    gscore = []
    for g in range(N_GROUP):
        cg = choice[g * GROUP_SIZE:(g + 1) * GROUP_SIZE, :]
        m1, i1 = _first_argmax(cg, grow, GROUP_SIZE)
        m2 = jnp.max(jnp.where(grow == i1, neg_inf, cg), axis=0, keepdims=True)
        gscore.append(m1 + m2)
    masked = []
    for g in range(N_GROUP):
        beaten = jnp.zeros((1, tt), jnp.int32)
        for o in range(N_GROUP):
            if o == g:
                continue
            wins = (gscore[o] >= gscore[g]) if o < g else (gscore[o] > gscore[g])
            beaten = beaten + wins.astype(jnp.int32)
        keep = beaten < TOPK_GROUP
        masked.append(jnp.where(keep, choice[g * GROUP_SIZE:(g + 1) * GROUP_SIZE, :], neg_inf))
    cur = jnp.concatenate(masked, axis=0)

    sel_idx, sel_score = [], []
    for _ in range(TOP_K):
        _, idx = _first_argmax(cur, row, N_EXPERTS)
        hit = row == idx
        sel_idx.append(idx)
        sel_score.append(jnp.sum(jnp.where(hit, scores, 0.0), axis=0, keepdims=True))
        cur = jnp.where(hit, neg_inf, cur)
    onehot = jnp.zeros((N_EXPERTS, tt), f32)
    for idx in sel_idx:
        onehot = onehot + (row == idx).astype(f32)

    upper = (lax.broadcasted_iota(jnp.int32, (tt, tt), 0) < lax.broadcasted_iota(jnp.int32, (tt, tt), 1))
    before = _dot(onehot.astype(bf16), upper.astype(bf16)) + carry_ref[...]
    carry_ref[...] = carry_ref[...] + jnp.sum(onehot, axis=1, keepdims=True)
    count_ref[...] = carry_ref[...].astype(jnp.int32)

    denom = sel_score[0]
    for s in sel_score[1:]:
        denom = denom + s
    for k in range(TOP_K):
        hit = row == sel_idx[k]
        e_ref[k:k + 1, :] = sel_idx[k]
        rank_ref[k:k + 1, :] = jnp.sum(jnp.where(hit, before, 0.0), axis=0, keepdims=True).astype(jnp.int32)
        gw_ref[k:k + 1, :] = sel_score[k] / denom * ROUTED_SCALE


def _route(logits_t, router_bias):
    T = logits_t.shape[1]
    tt = ROUTE_TT
    slot = pl.BlockSpec((TOP_K, tt), lambda i: (0, i))
    col = pl.BlockSpec((N_EXPERTS, 1), lambda i: (0, 0))
    return pl.pallas_call(
        _router_kernel,
        grid=(T // tt,),
        in_specs=[pl.BlockSpec((N_EXPERTS, tt), lambda i: (0, i)), col],
        out_specs=[slot, slot, slot, col],
        out_shape=[jax.ShapeDtypeStruct((TOP_K, T), jnp.int32), jax.ShapeDtypeStruct((TOP_K, T), jnp.int32),
                   jax.ShapeDtypeStruct((TOP_K, T), f32), jax.ShapeDtypeStruct((N_EXPERTS, 1), jnp.int32)],
        scratch_shapes=[pltpu.VMEM((N_EXPERTS, 1), f32)],
        compiler_params=pltpu.CompilerParams(dimension_semantics=("arbitrary",), vmem_limit_bytes=VMEM_LIMIT),
        name="router",
    )(logits_t, router_bias.reshape(N_EXPERTS, 1))


def _dest_kernel(e_ref, rank_ref, start_ref, dest_ref):
    tt = ROUTE_TT
    row = lax.broadcasted_iota(jnp.int32, (N_EXPERTS, tt), 0)
    start = start_ref[...]
    for k in range(TOP_K):
        hit = row == e_ref[k:k + 1, :]
        base = jnp.sum(jnp.where(hit, start, 0.0), axis=0, keepdims=True)
        dest_ref[k:k + 1, :] = base.astype(jnp.int32) + rank_ref[k:k + 1, :]


def _destinations(e_t, rank_t, start):
    T = e_t.shape[1]
    tt = ROUTE_TT
    assert T * TOP_K < 2 ** 24
    slot = pl.BlockSpec((TOP_K, tt), lambda i: (0, i))
    return pl.pallas_call(
        _dest_kernel,
        grid=(T // tt,),
        in_specs=[slot, slot, pl.BlockSpec((N_EXPERTS, 1), lambda i: (0, 0))],
        out_specs=slot,
        out_shape=jax.ShapeDtypeStruct((TOP_K, T), jnp.int32),
        compiler_params=pltpu.CompilerParams(dimension_semantics=("parallel",)),
        name="destinations",
    )(e_t, rank_t, start.astype(f32).reshape(N_EXPERTS, 1))


def _row_tile(ref, idx):
    return ref.at[pl.ds(pl.multiple_of(idx * ROW_TILE, ROW_TILE), ROW_TILE), :]


def _dispatch_kernel(dest_ref, h_ref, xs_hbm, sem):
    tt = DISPATCH_TT

    def issue(j, carry):
        src = _row_tile(h_ref, j)
        for k in range(TOP_K):
            pltpu.make_async_copy(src, _row_tile(xs_hbm, dest_ref[k, j]), sem).start(priority=k % 2)
        return carry

    lax.fori_loop(0, tt, issue, 0, unroll=ROW_LOOP_UNROLL)
    for _ in range(TOP_K):
        pltpu.make_async_copy(h_ref, xs_hbm.at[pl.ds(0, tt * ROW_TILE), :], sem).wait()


def _dispatch(dest_t, h_rows):
    T = dest_t.shape[1]
    tt = DISPATCH_TT
    slot = pl.BlockSpec((TOP_K, tt), lambda i: (0, i), memory_space=pltpu.SMEM)
    return pl.pallas_call(
        _dispatch_kernel,
        grid=(T // tt,),
        in_specs=[slot, pl.BlockSpec((tt * ROW_TILE, LANES), lambda i: (i, 0))],
        out_specs=pl.BlockSpec(memory_space=pl.ANY),
        scratch_shapes=[pltpu.SemaphoreType.DMA(())],
        out_shape=jax.ShapeDtypeStruct((T * TOP_K * ROW_TILE, LANES), f32),
        compiler_params=pltpu.CompilerParams(dimension_semantics=("arbitrary",)),
        name="dispatch",
    )(dest_t, h_rows)


def _moe_kernel(tile_ref, exp_ref, valid_ref, start_ref, end_ref, x_ref, w1_ref, w3_ref, w2_ref, y_ref,
                xb_ref, w1b_ref, w3b_ref, w2b_ref):
    bm = MOE_BM
    i = pl.program_id(0)
    prev = jnp.maximum(i - 1, 0)
    e, tile = exp_ref[i], tile_ref[i]
    new_expert = (i == 0) | (e != exp_ref[prev])
    new_tile = (i == 0) | (tile != tile_ref[prev])
    row0 = tile * bm
    shared_tile = (start_ref[e] > row0) | (end_ref[e] < row0 + bm)
    valid = valid_ref[i] == 1

    @pl.when(new_expert)
    def _():
        w1b_ref[...] = w1_ref[...].astype(bf16)
        w3b_ref[...] = w3_ref[...].astype(bf16)
        w2b_ref[...] = w2_ref[...].astype(bf16)

    @pl.when(new_tile)
    def _():
        for s in range(ROW_TILE):
            xb_ref[:, s * LANES:(s + 1) * LANES] = x_ref[pl.ds(s, bm, stride=ROW_TILE), :].astype(bf16)

    @pl.when(new_tile & shared_tile)
    def _():
        y_ref[...] = jnp.zeros_like(y_ref)

    def expert_rows():
        xb = xb_ref[...]
        act = (_silu(_dot(xb, w1b_ref[...])) * _dot(xb, w3b_ref[...])).astype(bf16)
        y = _dot(act, w2b_ref[...])
        return [y[:, s * LANES:(s + 1) * LANES] for s in range(ROW_TILE)]

    @pl.when(valid & jnp.logical_not(shared_tile))
    def _():
        for s, cols in enumerate(expert_rows()):
            y_ref[pl.ds(s, bm, stride=ROW_TILE), :] = cols

    @pl.when(valid & shared_tile)
    def _():
        row = row0 + lax.broadcasted_iota(jnp.int32, (bm, 1), 0)
        mine = (row >= start_ref[e]) & (row < end_ref[e])
        for s, cols in enumerate(expert_rows()):
            rows = pl.ds(s, bm, stride=ROW_TILE)
            y_ref[rows, :] = jnp.where(mine, cols, y_ref[rows, :])


def _work_items(counts, n_rows):
    bm = MOE_BM
    n_tiles = n_rows // bm
    n_items = n_tiles + N_EXPERTS
    end = jnp.cumsum(counts)
    start = end - counts
    first_tile = start // bm
    last_tile = jnp.maximum(end - 1, 0) // bm
    per_expert = jnp.where(counts > 0, last_tile - first_tile + 1, 0)
    item_end = jnp.cumsum(per_expert)
    item_start = item_end - per_expert
    ids = jnp.arange(n_items, dtype=jnp.int32)
    used = item_end[-1]
    valid = ids < used
    ids_c = jnp.minimum(ids, used - 1)
    exp_of = jnp.sum(item_end[None, :] <= ids_c[:, None], axis=1).astype(jnp.int32)
    pick = exp_of[:, None] == jnp.arange(N_EXPERTS, dtype=jnp.int32)[None, :]
    tile_of = jnp.sum(jnp.where(pick, (first_tile - item_start)[None, :], 0), axis=1) + ids_c
    return (tile_of.astype(jnp.int32), exp_of, valid.astype(jnp.int32), start.astype(jnp.int32),
            end.astype(jnp.int32))


def _experts(items, xs, w1, w3, w2):
    bm = MOE_BM
    tile_of, exp_of, valid, start, end = items
    n_items = tile_of.shape[0]
    rows = pl.BlockSpec((bm * ROW_TILE, LANES), lambda i, t, e, *_: (t[i], 0))
    wspec = lambda a: pl.BlockSpec((None, None) + a.shape[2:], lambda i, t, e, *_: (0, e[i], 0, 0))
    return pl.pallas_call(
        _moe_kernel,
        grid_spec=pltpu.PrefetchScalarGridSpec(
            num_scalar_prefetch=5,
            grid=(n_items,),
            in_specs=[rows, wspec(w1), wspec(w3), wspec(w2)],
            out_specs=rows,
            scratch_shapes=[pltpu.VMEM((bm, D_MODEL), bf16), pltpu.VMEM(w1.shape[2:], bf16),
                            pltpu.VMEM(w3.shape[2:], bf16), pltpu.VMEM(w2.shape[2:], bf16)],
        ),
        out_shape=jax.ShapeDtypeStruct(xs.shape, f32),
        compiler_params=pltpu.CompilerParams(dimension_semantics=("arbitrary",), vmem_limit_bytes=VMEM_LIMIT),
        name="experts",
    )(tile_of, exp_of, valid, start, end, xs, w1, w3, w2)


def _combine_kernel(dest_ref, gw_ref, dest_next_ref, y_hbm, base_ref, g2_ref, b2_ref, o_ref,
                    buf_ref, acc_ref, sem):
    tt = COMBINE_TT
    i = pl.program_id(0)
    cur = i % 2

    def gather(d_ref, slot):
        def issue(j, carry):
            for k in range(TOP_K):
                pltpu.make_async_copy(_row_tile(y_hbm, d_ref[k, j]), _row_tile(buf_ref.at[slot, k], j),
                                      sem.at[slot]).start(priority=k % 2)
            return carry

        lax.fori_loop(0, tt, issue, 0, unroll=ROW_LOOP_UNROLL)

    @pl.when(i == 0)
    def _():
        gather(dest_ref, 0)

    @pl.when(i + 1 < pl.num_programs(0))
    def _():
        gather(dest_next_ref, 1 - cur)

    for k in range(TOP_K):
        pltpu.make_async_copy(y_hbm.at[pl.ds(0, tt * ROW_TILE), :], buf_ref.at[cur, k], sem.at[cur]).wait()

    def weigh(j, carry):
        rows = pl.ds(pl.multiple_of(j * ROW_TILE, ROW_TILE), ROW_TILE)
        tot = gw_ref[0, j] * buf_ref[cur, 0, rows, :]
        for k in range(1, TOP_K):
            tot = tot + gw_ref[k, j] * buf_ref[cur, k, rows, :]
        acc_ref[rows, :] = tot
        return carry

    lax.fori_loop(0, tt, weigh, 0, unroll=ROW_LOOP_UNROLL)
    routed = jnp.concatenate([acc_ref[pl.ds(s, tt, stride=ROW_TILE), :] for s in range(ROW_TILE)], axis=1)
    o_ref[...] = _layernorm(base_ref[...] + routed, g2_ref[...], b2_ref[...])


def _combine(dest_t, gw_t, y_rows, base, ln2_g, ln2_b):
    T = base.shape[0]
    tt = COMBINE_TT
    n_steps = T // tt
    slot = pl.BlockSpec((TOP_K, tt), lambda i: (0, i), memory_space=pltpu.SMEM)
    slot_next = pl.BlockSpec((TOP_K, tt), lambda i: (0, jnp.minimum(i + 1, n_steps - 1)), memory_space=pltpu.SMEM)
    vec = pl.BlockSpec((1, D_MODEL), lambda i: (0, 0))
    rows = pl.BlockSpec((tt, D_MODEL), lambda i: (i, 0))
    return pl.pallas_call(
        _combine_kernel,
        grid=(n_steps,),
        in_specs=[slot, slot, slot_next, pl.BlockSpec(memory_space=pl.ANY), rows, vec, vec],
        out_specs=rows,
        scratch_shapes=[pltpu.VMEM((2, TOP_K, tt * ROW_TILE, LANES), f32), pltpu.VMEM((tt * ROW_TILE, LANES), f32),
                        pltpu.SemaphoreType.DMA((2,))],
        out_shape=jax.ShapeDtypeStruct((T, D_MODEL), f32),
        compiler_params=pltpu.CompilerParams(dimension_semantics=("arbitrary",), vmem_limit_bytes=VMEM_LIMIT),
        name="combine",
    )(dest_t, gw_t, dest_t, y_rows, base, ln2_g.reshape(1, -1), ln2_b.reshape(1, -1))


def _layer(x, w_in, g_q, w_uq, g_kv, w_ukv, g_out_dil, g_out_mla, w_o, ln1_g, ln1_b,
           router_w, router_bias, w1, w3, w2, ws1, ws3, ws2, ln2_g, ln2_b, layer):
    batch, seq, _ = x.shape
    T = batch * seq
    x2d = x.reshape(T, D_MODEL)
    qd, kd, vd, qm, km, vm = _project(x2d, w_in[layer], w_uq[layer], w_ukv[layer], g_q[layer], g_kv[layer], seq)
    o_dil = _dilated_attention(qd, kd, vd, batch, seq)
    o_mla = _mla_attention(qm, km, vm, batch, seq)
    base, h_rows, logits_t = _post_attention(
        x2d, o_dil, o_mla, g_out_dil[layer], g_out_mla[layer], w_o[layer], ln1_g[layer], ln1_b[layer],
        router_w[layer], ws1[layer], ws3[layer], ws2[layer])
    e_t, rank_t, gw_t, counts = _route(logits_t, router_bias[layer])
    items = _work_items(counts.reshape(N_EXPERTS), T * TOP_K)
    dest_t = _destinations(e_t, rank_t, items[3])
    xs = _dispatch(dest_t, h_rows)
    y_rows = _experts(items, xs, w1[layer:layer + 1], w3[layer:layer + 1], w2[layer:layer + 1])
    out = _combine(dest_t, gw_t, y_rows, base, ln2_g[layer], ln2_b[layer])
    return out.reshape(batch, seq, D_MODEL)


def kernel(x, w_in, g_q, w_uq, g_kv, w_ukv, g_out_dil, g_out_mla, w_o, ln1_g, ln1_b, router_w, router_bias,
           w1, w3, w2, ws1, ws3, ws2, ln2_g, ln2_b):
    assert x.shape[1] % DIL_TQ == 0 and x.shape[2] == D_MODEL
    h = x
    for layer in range(w_in.shape[0]):
        h = _layer(h, w_in, g_q, w_uq, g_kv, w_ukv, g_out_dil, g_out_mla, w_o, ln1_g, ln1_b,
                   router_w, router_bias, w1, w3, w2, ws1, ws3, ws2, ln2_g, ln2_b, layer)
    return h
```

```python
import functools

import jax
import jax.numpy as jnp
from jax import lax
from jax.experimental import pallas as pl
from jax.experimental.pallas import tpu as pltpu

D_MODEL = 1024
HEAD_DIM = 64
N_HEADS_DIL = 8
DIL_PATTERNS = ((128, 1), (512, 4), (2048, 16))
ROPE_DIM_DIL = HEAD_DIM // 4
ROPE_THETA = 500000.0
ATTN_BLOCK = 128
N_HEADS_MLA = 8
Q_LORA = 384
KV_LORA = 128
QK_NOPE = 64
QK_ROPE = 32
V_HEAD = 64
DIL_WIDTH = N_HEADS_DIL * HEAD_DIM
MLA_WIDTH = N_HEADS_MLA * V_HEAD
N_EXPERTS = 256
TOP_K = 8
N_GROUP = 8
TOPK_GROUP = 4
GROUP_SIZE = N_EXPERTS // N_GROUP
EXPERT_FF = 256
SHARED_FF = 256
ROUTED_SCALE = 2.5
DEPTH = 1
DEEPNORM_ALPHA = (2.0 * DEPTH) ** 0.25
LN_EPS = 1e-5
RMS_EPS = 1e-6
NEG = -1e30
LOG2E = 1.4426950408889634

LANES = 128
SUBLANES = 8
VMEM_LIMIT = 56 * 1024 * 1024

PROJ_TM = 512
DIL_TQ = 2048
MLA_TQ = 512
MLA_TK = 512
POST_TM = 512
ROUTE_TT = 512
DEST_TT = 2048
DISPATCH_TT = 1024
MOE_BM = 512
COMBINE_TT = 256
ROW_LOOP_UNROLL = 4

MLA_HEAD_PAD = LANES

bf16 = jnp.bfloat16
f32 = jnp.float32


def _dot(a, b):
    return jnp.dot(a, b, preferred_element_type=f32)


def _dot_nt(a, b):
    return lax.dot_general(a, b, (((1,), (1,)), ((), ())), preferred_element_type=f32)


def _rope_lanes(x, tab_ref, shift):
    return (x * tab_ref[0]
            + pltpu.roll(x, shift, 1) * tab_ref[1]
            + pltpu.roll(x, LANES - shift, 1) * tab_ref[2])


def _rms(x, g):
    return x * lax.rsqrt(jnp.mean(x * x, axis=-1, keepdims=True) + RMS_EPS) * g


def _layernorm(x, g, b):
    mu = jnp.mean(x, axis=-1, keepdims=True)
    xc = x - mu
    var = jnp.mean(xc * xc, axis=-1, keepdims=True)
    return xc * lax.rsqrt(var + LN_EPS) * g + b


def _silu(x):
    return x * (1.0 / (1.0 + jnp.exp(-x)))


ROW_TILE = D_MODEL // LANES
assert ROW_TILE == SUBLANES


def _proj_kernel(x_ref, win_ref, wuq_ref, wukv_ref, gq_ref, gkv_ref, ropd_ref, ropm_ref,
                 qd_ref, kd_ref, vd_ref, qm_ref, km_ref, vm_ref):
    xb = x_ref[...].astype(bf16)
    o_k, o_v, o_cq, o_ckv, o_kpe = DIL_WIDTH, 2 * DIL_WIDTH, 3 * DIL_WIDTH, 3 * DIL_WIDTH + Q_LORA, \
        3 * DIL_WIDTH + Q_LORA + KV_LORA
    rope_shift_d = ROPE_DIM_DIL // 2
    rope_shift_m = QK_ROPE // 2

    q = _dot(xb, win_ref[:, 0:o_k])
    k = _dot(xb, win_ref[:, o_k:o_v])
    for j in range(DIL_WIDTH // LANES):
        sl = slice(j * LANES, (j + 1) * LANES)
        qd_ref[:, sl] = _rope_lanes(q[:, sl], ropd_ref, rope_shift_d) * (HEAD_DIM ** -0.5 * LOG2E)
        kd_ref[:, sl] = _rope_lanes(k[:, sl], ropd_ref, rope_shift_d)
    vd_ref[...] = _dot(xb, win_ref[:, o_v:o_cq])

    cq = _rms(_dot(xb, win_ref[:, o_cq:o_ckv]), gq_ref[...]).astype(bf16)
    qm = _dot(cq, wuq_ref[...])
    ckv = _rms(_dot(xb, win_ref[:, o_ckv:o_kpe]), gkv_ref[...]).astype(bf16)
    kn = _dot(ckv, wukv_ref[:, 0:N_HEADS_MLA * MLA_HEAD_PAD])
    kpe = _rope_lanes(_dot(xb, win_ref[:, o_kpe:o_kpe + LANES]), ropm_ref, rope_shift_m)
    vv = _dot(ckv, wukv_ref[:, N_HEADS_MLA * MLA_HEAD_PAD:])
    scale = (QK_NOPE + QK_ROPE) ** -0.5 * LOG2E
    ones_pad = (lax.broadcasted_iota(jnp.int32, (1, LANES), 1) >= V_HEAD).astype(f32)
    for h in range(N_HEADS_MLA):
        sl = slice(h * LANES, (h + 1) * LANES)
        qm_ref[:, sl] = (_rope_lanes(qm[:, sl], ropm_ref, rope_shift_m) * scale).astype(bf16)
        km_ref[:, sl] = (kn[:, sl] + kpe).astype(bf16)
        vm_ref[:, sl] = (vv[:, sl] + ones_pad).astype(bf16)


def _rope_tables(seq):
    pos = jnp.arange(seq, dtype=f32)[:, None]
    lane = jnp.arange(LANES)

    def build(dim, lane_in_head, period):
        half = dim // 2
        inv = ROPE_THETA ** (-jnp.arange(0, dim, 2, dtype=f32) / dim)
        ang = pos * inv[None, :]
        cos, sin = jnp.cos(ang), jnp.sin(ang)
        off = lane_in_head(lane % period)
        in_x1 = (off >= 0) & (off < half)
        in_x2 = (off >= half) & (off < dim)
        idx = jnp.clip(jnp.where(in_x2, off - half, off), 0, half - 1)
        c = jnp.where((in_x1 | in_x2)[None, :], cos[:, idx], 1.0)
        sa = jnp.where(in_x2[None, :], sin[:, idx], 0.0)
        sb = jnp.where(in_x1[None, :], -sin[:, idx], 0.0)
        return jnp.stack([c, sa, sb]).astype(f32)

    tab_d = build(ROPE_DIM_DIL, lambda l: jnp.where(l < ROPE_DIM_DIL, l, -1), HEAD_DIM)
    tab_m = build(QK_ROPE, lambda l: jnp.where((l >= QK_NOPE) & (l < QK_NOPE + QK_ROPE), l - QK_NOPE, -1), LANES)
    return tab_d, tab_m


def _project(x2d, w_in, w_uq, w_ukv, g_q, g_kv, seq):
    T = x2d.shape[0]
    tm = PROJ_TM
    n_seq_tiles = seq // tm
    tab_d, tab_m = _rope_tables(seq)

    kpe_cols = jnp.zeros((D_MODEL, LANES), f32).at[:, QK_NOPE:QK_NOPE + QK_ROPE].set(w_in[:, -QK_ROPE:])
    w_in_b = jnp.concatenate([w_in[:, :-QK_ROPE], kpe_cols], axis=1).astype(bf16)
    w_uq_b = jnp.pad(w_uq.reshape(Q_LORA, N_HEADS_MLA, QK_NOPE + QK_ROPE),
                     ((0, 0), (0, 0), (0, MLA_HEAD_PAD - QK_NOPE - QK_ROPE))
                     ).reshape(Q_LORA, N_HEADS_MLA * MLA_HEAD_PAD).astype(bf16)
    w_ukv3 = w_ukv.reshape(KV_LORA, N_HEADS_MLA, QK_NOPE + V_HEAD)
    w_uk = jnp.pad(w_ukv3[:, :, :QK_NOPE], ((0, 0), (0, 0), (0, MLA_HEAD_PAD - QK_NOPE))
                   ).reshape(KV_LORA, N_HEADS_MLA * MLA_HEAD_PAD)
    w_uv = jnp.pad(w_ukv3[:, :, QK_NOPE:], ((0, 0), (0, 0), (0, MLA_HEAD_PAD - V_HEAD))
                   ).reshape(KV_LORA, N_HEADS_MLA * MLA_HEAD_PAD)
    w_ukv_b = jnp.concatenate([w_uk, w_uv], axis=1).astype(bf16)

    full = lambda a: pl.BlockSpec(a.shape, lambda i: (0,) * a.ndim)
    rows = lambda w: pl.BlockSpec((tm, w), lambda i: (i, 0))
    tab_spec = pl.BlockSpec((3, tm, LANES), lambda i: (0, i % n_seq_tiles, 0))
    gq2, gkv2 = g_q.reshape(1, Q_LORA), g_kv.reshape(1, KV_LORA)
    mla_w = N_HEADS_MLA * MLA_HEAD_PAD
    return pl.pallas_call(
        _proj_kernel,
        grid=(T // tm,),
        in_specs=[rows(D_MODEL), full(w_in_b), full(w_uq_b), full(w_ukv_b), full(gq2), full(gkv2),
                  tab_spec, tab_spec],
        out_specs=[rows(DIL_WIDTH), rows(DIL_WIDTH), rows(DIL_WIDTH), rows(mla_w), rows(mla_w), rows(mla_w)],
        out_shape=[jax.ShapeDtypeStruct((T, DIL_WIDTH), f32)] * 3 + [jax.ShapeDtypeStruct((T, mla_w), bf16)] * 3,
        compiler_params=pltpu.CompilerParams(dimension_semantics=("parallel",), vmem_limit_bytes=VMEM_LIMIT),
        name="proj",
    )(x2d, w_in_b, w_uq_b, w_ukv_b, gq2, gkv2, tab_d, tab_m)


def _dilated_kernel(q_ref, kc_ref, kp_ref, vc_ref, vp_ref, o_ref, pv_ref, m_ref):
    blk = ATTN_BLOCK
    has_prev_tile = pl.program_id(2) > 0
    lane = lax.broadcasted_iota(jnp.int32, (1, LANES), 1)
    head_lanes = (lane < HEAD_DIM, lane >= HEAD_DIM)
    qi = lax.broadcasted_iota(jnp.int32, (blk, 2 * blk), 0)
    ki = lax.broadcasted_iota(jnp.int32, (blk, 2 * blk), 1)
    band = (ki >= qi) & (ki <= qi + blk)
    band_seq_start = band & ((ki >= blk) | has_prev_tile)

    for p_idx, (window, d) in enumerate(DIL_PATTERNS):
        assert window // d == blk and DIL_TQ % (d * blk) == 0
        n_blk = DIL_TQ // (d * blk)
        for r in range(d):
            for b in range(n_blk):
                def strided(bb):
                    return pl.ds(r + d * blk * bb, blk, stride=d) if d > 1 else pl.ds(blk * bb, blk)
                rows = strided(b)
                if b > 0:
                    kp, vp, mask = kc_ref[0, strided(b - 1), :], vc_ref[0, strided(b - 1), :], band
                else:
                    kp, vp = kp_ref[0, strided(n_blk - 1), :], vp_ref[0, strided(n_blk - 1), :]
                    mask = band_seq_start
                q = q_ref[0, rows, :]
                kcat = jnp.concatenate([kp, kc_ref[0, rows, :]], axis=0).astype(bf16)
                vcat = jnp.concatenate([vp, vc_ref[0, rows, :]], axis=0)
                for h in range(2):
                    qh = jnp.where(head_lanes[h], q, 0.0).astype(bf16)
                    s = jnp.where(mask, _dot_nt(qh, kcat), NEG)
                    mb = jnp.max(s, axis=1, keepdims=True)
                    p = jnp.exp2(s - mb).astype(bf16)
                    vh = jnp.where(head_lanes[h], vcat, 1.0).astype(bf16)
                    pv_ref[p_idx, h, rows, :] = _dot(p, vh)
                    m_ref[p_idx, h, rows, :] = jnp.broadcast_to(mb, (blk, LANES))

    n_pat = len(DIL_PATTERNS)
    chunk = 2 * blk

    def merge(c, carry):
        rows = pl.ds(pl.multiple_of(c * chunk, chunk), chunk)
        outs = []
        for h in range(2):
            ms = [m_ref[p, h, rows, :] for p in range(n_pat)]
            m_all = functools.reduce(jnp.maximum, ms)
            tot = sum(jnp.exp2(ms[p] - m_all) * pv_ref[p, h, rows, :] for p in range(n_pat))
            outs.append(tot / pltpu.roll(tot, HEAD_DIM, 1))
        o_ref[0, rows, :] = jnp.where(head_lanes[0], outs[0], outs[1]).astype(o_ref.dtype)
        return carry

    lax.fori_loop(0, DIL_TQ // chunk, merge, 0)


def _dilated_attention(qd, kd, vd, batch, seq):
    tq = DIL_TQ
    q3, k3, v3 = (a.reshape(batch, seq, DIL_WIDTH) for a in (qd, kd, vd))
    cur = pl.BlockSpec((1, tq, LANES), lambda b, hp, i: (b, i, hp))
    prev = pl.BlockSpec((1, tq, LANES), lambda b, hp, i: (b, jnp.maximum(i - 1, 0), hp))
    out = pl.pallas_call(
        _dilated_kernel,
        grid=(batch, DIL_WIDTH // LANES, seq // tq),
        in_specs=[cur, cur, prev, cur, prev],
        out_specs=cur,
        out_shape=jax.ShapeDtypeStruct((batch, seq, DIL_WIDTH), bf16),
        scratch_shapes=[pltpu.VMEM((len(DIL_PATTERNS), 2, tq, LANES), f32)] * 2,
        compiler_params=pltpu.CompilerParams(dimension_semantics=("parallel", "parallel", "arbitrary"),
                                             vmem_limit_bytes=VMEM_LIMIT),
        name="dilated_attn",
    )(q3, k3, k3, v3, v3)
    return out.reshape(batch * seq, DIL_WIDTH)


def _mla_kernel(q_ref, k_ref, v_ref, o_ref, acc_ref, m_ref):
    tq, tk = MLA_TQ, MLA_TK
    lane = lax.broadcasted_iota(jnp.int32, (1, LANES), 1)

    def query_tile(i, carry):
        q_rows = pl.ds(pl.multiple_of(i * tq, tq), tq)
        m_ref[...] = jnp.full(m_ref.shape, NEG, f32)
        acc_ref[...] = jnp.zeros(acc_ref.shape, f32)

        def step(j, n_tiles, masked):
            nk = n_tiles * tk
            rows = pl.ds(pl.multiple_of(j * tk, tk), nk)
            if masked:
                col = lax.broadcasted_iota(jnp.int32, (tq, nk), 1)
                causal = col <= lax.broadcasted_iota(jnp.int32, (tq, nk), 0) + (nk - tk)
            for h in range(2):
                hs = slice(h * LANES, (h + 1) * LANES)
                s = _dot_nt(q_ref[0, q_rows, hs], k_ref[0, rows, hs])
                if masked:
                    s = jnp.where(causal, s, NEG)
                m_old = m_ref[h]
                m_new = jnp.maximum(m_old, jnp.max(s, axis=1, keepdims=True))
                p = jnp.concatenate([jnp.exp2(s[:, c * LANES:(c + 1) * LANES] - m_new)
                                     for c in range(nk // LANES)], axis=1).astype(bf16)
                acc_ref[h] = acc_ref[h] * jnp.exp2(m_old - m_new) + _dot(p, v_ref[0, rows, hs])
                m_ref[h] = m_new

        def body(j2, c):
            step(2 * j2, 2, False)
            return c

        lax.fori_loop(0, i // 2, body, 0)

        @pl.when(i % 2 == 0)
        def _():
            step(i, 1, True)

        @pl.when(i % 2 == 1)
        def _():
            step(i - 1, 2, True)

        acc0, acc1 = acc_ref[0], acc_ref[1]
        out = jnp.where(lane < V_HEAD, acc0 / pltpu.roll(acc0, V_HEAD, 1), pltpu.roll(acc1, V_HEAD, 1) / acc1)
        o_ref[0, q_rows, :] = out.astype(o_ref.dtype)
        return carry

    lax.fori_loop(0, q_ref.shape[1] // tq, query_tile, 0)


def _mla_attention(qm, km, vm, batch, seq):
    assert MLA_TQ == MLA_TK
    tq = MLA_TQ
    mla_w = N_HEADS_MLA * MLA_HEAD_PAD
    q3 = qm.reshape(batch, seq, mla_w)
    k3 = km.reshape(batch, seq, mla_w)
    v3 = vm.reshape(batch, seq, mla_w)
    out = pl.pallas_call(
        _mla_kernel,
        grid=(batch, N_HEADS_MLA // 2),
        in_specs=[pl.BlockSpec((1, seq, 2 * LANES), lambda b, hp: (b, 0, hp))] * 3,
        out_specs=pl.BlockSpec((1, seq, LANES), lambda b, hp: (b, 0, hp)),
        out_shape=jax.ShapeDtypeStruct((batch, seq, MLA_WIDTH), bf16),
        scratch_shapes=[pltpu.VMEM((2, tq, LANES), f32), pltpu.VMEM((2, tq, LANES), f32)],
        compiler_params=pltpu.CompilerParams(dimension_semantics=("parallel", "parallel"),
                                             vmem_limit_bytes=VMEM_LIMIT),
        name="mla_attn",
    )(q3, k3, v3)
    return out.reshape(batch * seq, MLA_WIDTH)


def _post_attn_kernel(x_ref, od_ref, om_ref, gd_ref, gm_ref, wo_ref, g1_ref, b1_ref, rw_ref,
                      ws1_ref, ws3_ref, ws2_ref, base_ref, hrow_ref, logit_ref):
    tm = POST_TM
    yd = _rms(od_ref[...].astype(f32), gd_ref[...]).astype(bf16)
    ym = _rms(om_ref[...].astype(f32), gm_ref[...]).astype(bf16)
    mix = _dot(yd, wo_ref[0:DIL_WIDTH, :]) + _dot(ym, wo_ref[DIL_WIDTH:, :])
    h1 = _layernorm(DEEPNORM_ALPHA * x_ref[...] + mix, g1_ref[...], b1_ref[...])
    for s in range(ROW_TILE):
        hrow_ref[pl.ds(s, tm, stride=ROW_TILE), :] = h1[:, s * LANES:(s + 1) * LANES]
    hb = h1.astype(bf16)
    logit_ref[...] = _dot_nt(rw_ref[...], hb)
    act = (_silu(_dot(hb, ws1_ref[...])) * _dot(hb, ws3_ref[...])).astype(bf16)
    base_ref[...] = DEEPNORM_ALPHA * h1 + _dot(act, ws2_ref[...])


def _post_attention(x2d, o_dil, o_mla, g_out_dil, g_out_mla, w_o, ln1_g, ln1_b, router_w, ws1, ws3, ws2):
    T = x2d.shape[0]
    tm = POST_TM
    full = lambda a: pl.BlockSpec(a.shape, lambda i: (0,) * a.ndim)
    rows = lambda w: pl.BlockSpec((tm, w), lambda i: (i, 0))
    args = (x2d, o_dil, o_mla, g_out_dil.reshape(1, -1), g_out_mla.reshape(1, -1), w_o.astype(bf16),
            ln1_g.reshape(1, -1), ln1_b.reshape(1, -1), router_w.astype(bf16),
            ws1.astype(bf16), ws3.astype(bf16), ws2.astype(bf16))
    return pl.pallas_call(
        _post_attn_kernel,
        grid=(T // tm,),
        in_specs=[rows(D_MODEL), rows(DIL_WIDTH), rows(MLA_WIDTH)] + [full(a) for a in args[3:]],
        out_specs=[rows(D_MODEL), pl.BlockSpec((tm * ROW_TILE, LANES), lambda i: (i, 0)),
                   pl.BlockSpec((N_EXPERTS, tm), lambda i: (0, i))],
        out_shape=[jax.ShapeDtypeStruct((T, D_MODEL), f32),
                   jax.ShapeDtypeStruct((T * ROW_TILE, LANES), f32),
                   jax.ShapeDtypeStruct((N_EXPERTS, T), f32)],
        compiler_params=pltpu.CompilerParams(dimension_semantics=("parallel",), vmem_limit_bytes=VMEM_LIMIT),
        name="post_attn",
    )(*args)


def _first_argmax(vals, row_ids, n_rows):
    mx = jnp.max(vals, axis=0, keepdims=True)
    idx = jnp.min(jnp.where(vals == mx, row_ids, n_rows), axis=0, keepdims=True)
    return mx, idx


def _router_kernel(logit_ref, bias_ref, e_ref, rank_ref, gw_ref, count_ref, carry_ref):
    tt = ROUTE_TT
    step = pl.program_id(0)

    @pl.when(step == 0)
    def _():
        carry_ref[...] = jnp.zeros_like(carry_ref)

    scores = 1.0 / (1.0 + jnp.exp(-logit_ref[...]))
    choice = scores + bias_ref[...]
    row = lax.broadcasted_iota(jnp.int32, (N_EXPERTS, tt), 0)
    grow = lax.broadcasted_iota(jnp.int32, (GROUP_SIZE, tt), 0)
    neg_inf = -jnp.inf

    gscore = []
    for g in range(N_GROUP):
        cg = choice[g * GROUP_SIZE:(g + 1) * GROUP_SIZE, :]
        m1, i1 = _first_argmax(cg, grow, GROUP_SIZE)
        m2 = jnp.max(jnp.where(grow == i1, neg_inf, cg), axis=0, keepdims=True)
        gscore.append(m1 + m2)
    masked = []
    for g in range(N_GROUP):
        beaten = jnp.zeros((1, tt), jnp.int32)
        for o in range(N_GROUP):
            if o == g:
                continue
            wins = (gscore[o] >= gscore[g]) if o < g else (gscore[o] > gscore[g])
            beaten = beaten + wins.astype(jnp.int32)
        keep = beaten < TOPK_GROUP
        masked.append(jnp.where(keep, choice[g * GROUP_SIZE:(g + 1) * GROUP_SIZE, :], neg_inf))
    cur = jnp.concatenate(masked, axis=0)

    sel_idx, sel_score = [], []
    for _ in range(TOP_K):
        _, idx = _first_argmax(cur, row, N_EXPERTS)
        hit = row == idx
        sel_idx.append(idx)
        sel_score.append(jnp.sum(jnp.where(hit, scores, 0.0), axis=0, keepdims=True))
        cur = jnp.where(hit, neg_inf, cur)
    onehot = jnp.zeros((N_EXPERTS, tt), f32)
    for idx in sel_idx:
        onehot = onehot + (row == idx).astype(f32)

    upper = (lax.broadcasted_iota(jnp.int32, (tt, tt), 0) < lax.broadcasted_iota(jnp.int32, (tt, tt), 1))
    before = _dot(onehot.astype(bf16), upper.astype(bf16)) + carry_ref[...]
    carry_ref[...] = carry_ref[...] + jnp.sum(onehot, axis=1, keepdims=True)
    count_ref[...] = carry_ref[...].astype(jnp.int32)

    denom = sel_score[0]
    for s in sel_score[1:]:
        denom = denom + s
    for k in range(TOP_K):
        hit = row == sel_idx[k]
        e_ref[k:k + 1, :] = sel_idx[k]
        rank_ref[k:k + 1, :] = jnp.sum(jnp.where(hit, before, 0.0), axis=0, keepdims=True).astype(jnp.int32)
        gw_ref[k:k + 1, :] = sel_score[k] / denom * ROUTED_SCALE


def _route(logits_t, router_bias):
    T = logits_t.shape[1]
    tt = ROUTE_TT
    slot = pl.BlockSpec((TOP_K, tt), lambda i: (0, i))
    col = pl.BlockSpec((N_EXPERTS, 1), lambda i: (0, 0))
    return pl.pallas_call(
        _router_kernel,
        grid=(T // tt,),
        in_specs=[pl.BlockSpec((N_EXPERTS, tt), lambda i: (0, i)), col],
        out_specs=[slot, slot, slot, col],
        out_shape=[jax.ShapeDtypeStruct((TOP_K, T), jnp.int32), jax.ShapeDtypeStruct((TOP_K, T), jnp.int32),
                   jax.ShapeDtypeStruct((TOP_K, T), f32), jax.ShapeDtypeStruct((N_EXPERTS, 1), jnp.int32)],
        scratch_shapes=[pltpu.VMEM((N_EXPERTS, 1), f32)],
        compiler_params=pltpu.CompilerParams(dimension_semantics=("arbitrary",), vmem_limit_bytes=VMEM_LIMIT),
        name="router",
    )(logits_t, router_bias.reshape(N_EXPERTS, 1))


def _dest_kernel(e_ref, rank_ref, start_ref, dest_ref):
    tt = DEST_TT
    row = lax.broadcasted_iota(jnp.int32, (N_EXPERTS, tt), 0)
    start = start_ref[...]
    for k in range(TOP_K):
        hit = row == e_ref[k:k + 1, :]
        base = jnp.sum(jnp.where(hit, start, 0.0), axis=0, keepdims=True)
        dest_ref[k:k + 1, :] = base.astype(jnp.int32) + rank_ref[k:k + 1, :]


def _destinations(e_t, rank_t, start):
    T = e_t.shape[1]
    tt = DEST_TT
    assert T * TOP_K < 2 ** 24
    slot = pl.BlockSpec((TOP_K, tt), lambda i: (0, i))
    return pl.pallas_call(
        _dest_kernel,
        grid=(T // tt,),
        in_specs=[slot, slot, pl.BlockSpec((N_EXPERTS, 1), lambda i: (0, 0))],
        out_specs=slot,
        out_shape=jax.ShapeDtypeStruct((TOP_K, T), jnp.int32),
        compiler_params=pltpu.CompilerParams(dimension_semantics=("parallel",)),
        name="destinations",
    )(e_t, rank_t, start.astype(f32).reshape(N_EXPERTS, 1))


def _row_tile(ref, idx):
    return ref.at[pl.ds(pl.multiple_of(idx * ROW_TILE, ROW_TILE), ROW_TILE), :]


def _dispatch_kernel(dest_ref, h_ref, xs_hbm, sem):
    tt = DISPATCH_TT

    def issue(j, carry):
        src = _row_tile(h_ref, j)
        for k in range(TOP_K):
            pltpu.make_async_copy(src, _row_tile(xs_hbm, dest_ref[k, j]), sem).start(priority=k % 2)
        return carry

    lax.fori_loop(0, tt, issue, 0, unroll=ROW_LOOP_UNROLL)
    for _ in range(TOP_K):
        pltpu.make_async_copy(h_ref, xs_hbm.at[pl.ds(0, tt * ROW_TILE), :], sem).wait()


def _dispatch(dest_t, h_rows):
    T = dest_t.shape[1]
    tt = DISPATCH_TT
    slot = pl.BlockSpec((TOP_K, tt), lambda i: (0, i), memory_space=pltpu.SMEM)
    return pl.pallas_call(
        _dispatch_kernel,
        grid=(T // tt,),
        in_specs=[slot, pl.BlockSpec((tt * ROW_TILE, LANES), lambda i: (i, 0))],
        out_specs=pl.BlockSpec(memory_space=pl.ANY),
        scratch_shapes=[pltpu.SemaphoreType.DMA(())],
        out_shape=jax.ShapeDtypeStruct((T * TOP_K * ROW_TILE, LANES), f32),
        compiler_params=pltpu.CompilerParams(dimension_semantics=("arbitrary",)),
        name="dispatch",
    )(dest_t, h_rows)


def _moe_kernel(tile_ref, exp_ref, valid_ref, start_ref, end_ref, x_ref, w1_ref, w3_ref, w2_ref, y_ref,
                w1b_ref, w3b_ref, w2b_ref):
    bm = MOE_BM
    i = pl.program_id(0)
    prev = jnp.maximum(i - 1, 0)
    e, tile = exp_ref[i], tile_ref[i]
    new_expert = (i == 0) | (e != exp_ref[prev])
    new_tile = (i == 0) | (tile != tile_ref[prev])
    row0 = tile * bm
    shared_tile = (start_ref[e] > row0) | (end_ref[e] < row0 + bm)
    valid = valid_ref[i] == 1

    @pl.when(new_expert)
    def _():
        w1b_ref[...] = w1_ref[...].astype(bf16)
        w3b_ref[...] = w3_ref[...].astype(bf16)
        w2b_ref[...] = w2_ref[...].astype(bf16)

    @pl.when(new_tile & shared_tile)
    def _():
        y_ref[...] = jnp.zeros_like(y_ref)

    def expert_rows():
        xb = jnp.concatenate([x_ref[pl.ds(s, bm, stride=ROW_TILE), :].astype(bf16) for s in range(ROW_TILE)],
                             axis=1)
        act = (_silu(_dot(xb, w1b_ref[...])) * _dot(xb, w3b_ref[...])).astype(bf16)
        y = _dot(act, w2b_ref[...])
        return [y[:, s * LANES:(s + 1) * LANES] for s in range(ROW_TILE)]

    @pl.when(valid & jnp.logical_not(shared_tile))
    def _():
        for s, cols in enumerate(expert_rows()):
            y_ref[pl.ds(s, bm, stride=ROW_TILE), :] = cols

    @pl.when(valid & shared_tile)
    def _():
        row = row0 + lax.broadcasted_iota(jnp.int32, (bm, 1), 0)
        mine = (row >= start_ref[e]) & (row < end_ref[e])
        for s, cols in enumerate(expert_rows()):
            rows = pl.ds(s, bm, stride=ROW_TILE)
            y_ref[rows, :] = jnp.where(mine, cols, y_ref[rows, :])


def _work_items(counts, n_rows):
    bm = MOE_BM
    n_tiles = n_rows // bm
    n_items = n_tiles + N_EXPERTS
    end = jnp.cumsum(counts)
    start = end - counts
    first_tile = start // bm
    last_tile = jnp.maximum(end - 1, 0) // bm
    per_expert = jnp.where(counts > 0, last_tile - first_tile + 1, 0)
    item_end = jnp.cumsum(per_expert)
    item_start = item_end - per_expert
    ids = jnp.arange(n_items, dtype=jnp.int32)
    used = item_end[-1]
    valid = ids < used
    ids_c = jnp.minimum(ids, used - 1)
    exp_of = jnp.sum(item_end[None, :] <= ids_c[:, None], axis=1).astype(jnp.int32)
    pick = exp_of[:, None] == jnp.arange(N_EXPERTS, dtype=jnp.int32)[None, :]
    tile_of = jnp.sum(jnp.where(pick, (first_tile - item_start)[None, :], 0), axis=1) + ids_c
    return (tile_of.astype(jnp.int32), exp_of, valid.astype(jnp.int32), start.astype(jnp.int32),
            end.astype(jnp.int32))


def _experts(items, xs, w1, w3, w2):
    bm = MOE_BM
    tile_of, exp_of, valid, start, end = items
    n_items = tile_of.shape[0]
    rows = pl.BlockSpec((bm * ROW_TILE, LANES), lambda i, t, e, *_: (t[i], 0))
    wspec = lambda a: pl.BlockSpec((None, None) + a.shape[2:], lambda i, t, e, *_: (0, e[i], 0, 0))
    return pl.pallas_call(
        _moe_kernel,
        grid_spec=pltpu.PrefetchScalarGridSpec(
            num_scalar_prefetch=5,
            grid=(n_items,),
            in_specs=[rows, wspec(w1), wspec(w3), wspec(w2)],
            out_specs=rows,
            scratch_shapes=[pltpu.VMEM(w1.shape[2:], bf16),
                            pltpu.VMEM(w3.shape[2:], bf16), pltpu.VMEM(w2.shape[2:], bf16)],
        ),
        out_shape=jax.ShapeDtypeStruct(xs.shape, f32),
        compiler_params=pltpu.CompilerParams(dimension_semantics=("arbitrary",), vmem_limit_bytes=VMEM_LIMIT),
        name="experts",
    )(tile_of, exp_of, valid, start, end, xs, w1, w3, w2)


def _combine_kernel(dest_ref, gw_ref, dest_next_ref, y_hbm, base_ref, g2_ref, b2_ref, o_ref,
                    buf_ref, acc_ref, sem):
    tt = COMBINE_TT
    i = pl.program_id(0)
    cur = i % 2

    def gather(d_ref, slot):
        def issue(j, carry):
            for k in range(TOP_K):
                pltpu.make_async_copy(_row_tile(y_hbm, d_ref[k, j]), _row_tile(buf_ref.at[slot, k], j),
                                      sem.at[slot]).start(priority=k % 2)
            return carry

        lax.fori_loop(0, tt, issue, 0, unroll=ROW_LOOP_UNROLL)

    @pl.when(i == 0)
    def _():
        gather(dest_ref, 0)

    @pl.when(i + 1 < pl.num_programs(0))
    def _():
        gather(dest_next_ref, 1 - cur)

    for k in range(TOP_K):
        pltpu.make_async_copy(y_hbm.at[pl.ds(0, tt * ROW_TILE), :], buf_ref.at[cur, k], sem.at[cur]).wait()

    def weigh(j, carry):
        rows = pl.ds(pl.multiple_of(j * ROW_TILE, ROW_TILE), ROW_TILE)
        tot = gw_ref[0, j] * buf_ref[cur, 0, rows, :]
        for k in range(1, TOP_K):
            tot = tot + gw_ref[k, j] * buf_ref[cur, k, rows, :]
        acc_ref[rows, :] = tot
        return carry

    lax.fori_loop(0, tt, weigh, 0, unroll=ROW_LOOP_UNROLL)
    routed = jnp.concatenate([acc_ref[pl.ds(s, tt, stride=ROW_TILE), :] for s in range(ROW_TILE)], axis=1)
    o_ref[...] = _layernorm(base_ref[...] + routed, g2_ref[...], b2_ref[...])


def _combine(dest_t, gw_t, y_rows, base, ln2_g, ln2_b):
    T = base.shape[0]
    tt = COMBINE_TT
    n_steps = T // tt
    slot = pl.BlockSpec((TOP_K, tt), lambda i: (0, i), memory_space=pltpu.SMEM)
    slot_next = pl.BlockSpec((TOP_K, tt), lambda i: (0, jnp.minimum(i + 1, n_steps - 1)), memory_space=pltpu.SMEM)
    vec = pl.BlockSpec((1, D_MODEL), lambda i: (0, 0))
    rows = pl.BlockSpec((tt, D_MODEL), lambda i: (i, 0))
    return pl.pallas_call(
        _combine_kernel,
        grid=(n_steps,),
        in_specs=[slot, slot, slot_next, pl.BlockSpec(memory_space=pl.ANY), rows, vec, vec],
        out_specs=rows,
        scratch_shapes=[pltpu.VMEM((2, TOP_K, tt * ROW_TILE, LANES), f32), pltpu.VMEM((tt * ROW_TILE, LANES), f32),
                        pltpu.SemaphoreType.DMA((2,))],
        out_shape=jax.ShapeDtypeStruct((T, D_MODEL), f32),
        compiler_params=pltpu.CompilerParams(dimension_semantics=("arbitrary",), vmem_limit_bytes=VMEM_LIMIT),
        name="combine",
    )(dest_t, gw_t, dest_t, y_rows, base, ln2_g.reshape(1, -1), ln2_b.reshape(1, -1))


def _layer(x, w_in, g_q, w_uq, g_kv, w_ukv, g_out_dil, g_out_mla, w_o, ln1_g, ln1_b,
           router_w, router_bias, w1, w3, w2, ws1, ws3, ws2, ln2_g, ln2_b, layer):
    batch, seq, _ = x.shape
    T = batch * seq
    x2d = x.reshape(T, D_MODEL)
    qd, kd, vd, qm, km, vm = _project(x2d, w_in[layer], w_uq[layer], w_ukv[layer], g_q[layer], g_kv[layer], seq)
    o_dil = _dilated_attention(qd, kd, vd, batch, seq)
    o_mla = _mla_attention(qm, km, vm, batch, seq)
    base, h_rows, logits_t = _post_attention(
        x2d, o_dil, o_mla, g_out_dil[layer], g_out_mla[layer], w_o[layer], ln1_g[layer], ln1_b[layer],
        router_w[layer], ws1[layer], ws3[layer], ws2[layer])
    e_t, rank_t, gw_t, counts = _route(logits_t, router_bias[layer])
    items = _work_items(counts.reshape(N_EXPERTS), T * TOP_K)
    dest_t = _destinations(e_t, rank_t, items[3])
    xs = _dispatch(dest_t, h_rows)
    y_rows = _experts(items, xs, w1[layer:layer + 1], w3[layer:layer + 1], w2[layer:layer + 1])
    out = _combine(dest_t, gw_t, y_rows, base, ln2_g[layer], ln2_b[layer])
    return out.reshape(batch, seq, D_MODEL)


def kernel(x, w_in, g_q, w_uq, g_kv, w_ukv, g_out_dil, g_out_mla, w_o, ln1_g, ln1_b, router_w, router_bias,
           w1, w3, w2, ws1, ws3, ws2, ln2_g, ln2_b):
    assert x.shape[1] % DIL_TQ == 0 and x.shape[2] == D_MODEL
    h = x
    for layer in range(w_in.shape[0]):
        h = _layer(h, w_in, g_q, w_uq, g_kv, w_ukv, g_out_dil, g_out_mla, w_o, ln1_g, ln1_b,
                   router_w, router_bias, w1, w3, w2, ws1, ws3, ws2, ln2_g, ln2_b, layer)
    return h
```

```python
import functools

import jax
import jax.numpy as jnp
from jax import lax
from jax.experimental import pallas as pl
from jax.experimental.pallas import tpu as pltpu

D_MODEL = 1024
HEAD_DIM = 64
N_HEADS_DIL = 8
DIL_PATTERNS = ((128, 1), (512, 4), (2048, 16))
ROPE_DIM_DIL = HEAD_DIM // 4
ROPE_THETA = 500000.0
ATTN_BLOCK = 128
N_HEADS_MLA = 8
Q_LORA = 384
KV_LORA = 128
QK_NOPE = 64
QK_ROPE = 32
V_HEAD = 64
DIL_WIDTH = N_HEADS_DIL * HEAD_DIM
MLA_WIDTH = N_HEADS_MLA * V_HEAD
N_EXPERTS = 256
TOP_K = 8
N_GROUP = 8
TOPK_GROUP = 4
GROUP_SIZE = N_EXPERTS // N_GROUP
EXPERT_FF = 256
SHARED_FF = 256
ROUTED_SCALE = 2.5
DEPTH = 1
DEEPNORM_ALPHA = (2.0 * DEPTH) ** 0.25
LN_EPS = 1e-5
RMS_EPS = 1e-6
NEG = -1e30
LOG2E = 1.4426950408889634

LANES = 128
SUBLANES = 8
VMEM_LIMIT = 56 * 1024 * 1024

PROJ_TM = 512
DIL_TQ = 2048
MLA_TQ = 512
MLA_TK = 512
POST_TM = 512
ROUTE_TT = 512
DEST_TT = 2048
DISPATCH_TT = 1024
MOE_BM = 1024
COMBINE_TT = 512
ROW_LOOP_UNROLL = 4

MLA_HEAD_PAD = LANES

bf16 = jnp.bfloat16
f32 = jnp.float32


def _dot(a, b):
    return jnp.dot(a, b, preferred_element_type=f32)


def _dot_nt(a, b):
    return lax.dot_general(a, b, (((1,), (1,)), ((), ())), preferred_element_type=f32)


def _rope_lanes(x, tab_ref, shift):
    return (x * tab_ref[0]
            + pltpu.roll(x, shift, 1) * tab_ref[1]
            + pltpu.roll(x, LANES - shift, 1) * tab_ref[2])


def _rms(x, g):
    return x * lax.rsqrt(jnp.mean(x * x, axis=-1, keepdims=True) + RMS_EPS) * g


def _layernorm(x, g, b):
    mu = jnp.mean(x, axis=-1, keepdims=True)
    xc = x - mu
    var = jnp.mean(xc * xc, axis=-1, keepdims=True)
    return xc * lax.rsqrt(var + LN_EPS) * g + b


def _silu(x):
    return x * (1.0 / (1.0 + jnp.exp(-x)))


ROW_TILE = D_MODEL // LANES
assert ROW_TILE == SUBLANES


def _proj_kernel(x_ref, win_ref, wuq_ref, wukv_ref, gq_ref, gkv_ref, ropd_ref, ropm_ref,
                 qd_ref, kd_ref, vd_ref, qm_ref, km_ref, vm_ref):
    xb = x_ref[...].astype(bf16)
    o_k, o_v, o_cq, o_ckv, o_kpe = DIL_WIDTH, 2 * DIL_WIDTH, 3 * DIL_WIDTH, 3 * DIL_WIDTH + Q_LORA, \
        3 * DIL_WIDTH + Q_LORA + KV_LORA
    rope_shift_d = ROPE_DIM_DIL // 2
    rope_shift_m = QK_ROPE // 2

    q = _dot(xb, win_ref[:, 0:o_k])
    k = _dot(xb, win_ref[:, o_k:o_v])
    for j in range(DIL_WIDTH // LANES):
        sl = slice(j * LANES, (j + 1) * LANES)
        qd_ref[:, sl] = _rope_lanes(q[:, sl], ropd_ref, rope_shift_d) * (HEAD_DIM ** -0.5 * LOG2E)
        kd_ref[:, sl] = _rope_lanes(k[:, sl], ropd_ref, rope_shift_d)
    vd_ref[...] = _dot(xb, win_ref[:, o_v:o_cq])

    cq = _rms(_dot(xb, win_ref[:, o_cq:o_ckv]), gq_ref[...]).astype(bf16)
    qm = _dot(cq, wuq_ref[...])
    ckv = _rms(_dot(xb, win_ref[:, o_ckv:o_kpe]), gkv_ref[...]).astype(bf16)
    kn = _dot(ckv, wukv_ref[:, 0:N_HEADS_MLA * MLA_HEAD_PAD])
    kpe = _rope_lanes(_dot(xb, win_ref[:, o_kpe:o_kpe + LANES]), ropm_ref, rope_shift_m)
    vv = _dot(ckv, wukv_ref[:, N_HEADS_MLA * MLA_HEAD_PAD:])
    scale = (QK_NOPE + QK_ROPE) ** -0.5 * LOG2E
    ones_pad = (lax.broadcasted_iota(jnp.int32, (1, LANES), 1) >= V_HEAD).astype(f32)
    for h in range(N_HEADS_MLA):
        sl = slice(h * LANES, (h + 1) * LANES)
        qm_ref[:, sl] = (_rope_lanes(qm[:, sl], ropm_ref, rope_shift_m) * scale).astype(bf16)
        km_ref[:, sl] = (kn[:, sl] + kpe).astype(bf16)
        vm_ref[:, sl] = (vv[:, sl] + ones_pad).astype(bf16)


def _rope_tables(seq):
    pos = jnp.arange(seq, dtype=f32)[:, None]
    lane = jnp.arange(LANES)

    def build(dim, lane_in_head, period):
        half = dim // 2
        inv = ROPE_THETA ** (-jnp.arange(0, dim, 2, dtype=f32) / dim)
        ang = pos * inv[None, :]
        cos, sin = jnp.cos(ang), jnp.sin(ang)
        off = lane_in_head(lane % period)
        in_x1 = (off >= 0) & (off < half)
        in_x2 = (off >= half) & (off < dim)
        idx = jnp.clip(jnp.where(in_x2, off - half, off), 0, half - 1)
        c = jnp.where((in_x1 | in_x2)[None, :], cos[:, idx], 1.0)
        sa = jnp.where(in_x2[None, :], sin[:, idx], 0.0)
        sb = jnp.where(in_x1[None, :], -sin[:, idx], 0.0)
        return jnp.stack([c, sa, sb]).astype(f32)

    tab_d = build(ROPE_DIM_DIL, lambda l: jnp.where(l < ROPE_DIM_DIL, l, -1), HEAD_DIM)
    tab_m = build(QK_ROPE, lambda l: jnp.where((l >= QK_NOPE) & (l < QK_NOPE + QK_ROPE), l - QK_NOPE, -1), LANES)
    return tab_d, tab_m


def _project(x2d, w_in, w_uq, w_ukv, g_q, g_kv, seq):
    T = x2d.shape[0]
    tm = PROJ_TM
    n_seq_tiles = seq // tm
    tab_d, tab_m = _rope_tables(seq)

    kpe_cols = jnp.zeros((D_MODEL, LANES), f32).at[:, QK_NOPE:QK_NOPE + QK_ROPE].set(w_in[:, -QK_ROPE:])
    w_in_b = jnp.concatenate([w_in[:, :-QK_ROPE], kpe_cols], axis=1).astype(bf16)
    w_uq_b = jnp.pad(w_uq.reshape(Q_LORA, N_HEADS_MLA, QK_NOPE + QK_ROPE),
                     ((0, 0), (0, 0), (0, MLA_HEAD_PAD - QK_NOPE - QK_ROPE))
                     ).reshape(Q_LORA, N_HEADS_MLA * MLA_HEAD_PAD).astype(bf16)
    w_ukv3 = w_ukv.reshape(KV_LORA, N_HEADS_MLA, QK_NOPE + V_HEAD)
    w_uk = jnp.pad(w_ukv3[:, :, :QK_NOPE], ((0, 0), (0, 0), (0, MLA_HEAD_PAD - QK_NOPE))
                   ).reshape(KV_LORA, N_HEADS_MLA * MLA_HEAD_PAD)
    w_uv = jnp.pad(w_ukv3[:, :, QK_NOPE:], ((0, 0), (0, 0), (0, MLA_HEAD_PAD - V_HEAD))
                   ).reshape(KV_LORA, N_HEADS_MLA * MLA_HEAD_PAD)
    w_ukv_b = jnp.concatenate([w_uk, w_uv], axis=1).astype(bf16)

    full = lambda a: pl.BlockSpec(a.shape, lambda i: (0,) * a.ndim)
    rows = lambda w: pl.BlockSpec((tm, w), lambda i: (i, 0))
    tab_spec = pl.BlockSpec((3, tm, LANES), lambda i: (0, i % n_seq_tiles, 0))
    gq2, gkv2 = g_q.reshape(1, Q_LORA), g_kv.reshape(1, KV_LORA)
    mla_w = N_HEADS_MLA * MLA_HEAD_PAD
    return pl.pallas_call(
        _proj_kernel,
        grid=(T // tm,),
        in_specs=[rows(D_MODEL), full(w_in_b), full(w_uq_b), full(w_ukv_b), full(gq2), full(gkv2),
                  tab_spec, tab_spec],
        out_specs=[rows(DIL_WIDTH), rows(DIL_WIDTH), rows(DIL_WIDTH), rows(mla_w), rows(mla_w), rows(mla_w)],
        out_shape=[jax.ShapeDtypeStruct((T, DIL_WIDTH), f32)] * 3 + [jax.ShapeDtypeStruct((T, mla_w), bf16)] * 3,
        compiler_params=pltpu.CompilerParams(dimension_semantics=("parallel",), vmem_limit_bytes=VMEM_LIMIT),
        name="proj",
    )(x2d, w_in_b, w_uq_b, w_ukv_b, gq2, gkv2, tab_d, tab_m)


def _dilated_kernel(q_ref, kc_ref, kp_ref, vc_ref, vp_ref, o_ref, pv_ref, m_ref):
    blk = ATTN_BLOCK
    has_prev_tile = pl.program_id(2) > 0
    lane = lax.broadcasted_iota(jnp.int32, (1, LANES), 1)
    head_lanes = (lane < HEAD_DIM, lane >= HEAD_DIM)
    qi = lax.broadcasted_iota(jnp.int32, (blk, 2 * blk), 0)
    ki = lax.broadcasted_iota(jnp.int32, (blk, 2 * blk), 1)
    band = (ki >= qi) & (ki <= qi + blk)
    band_seq_start = band & ((ki >= blk) | has_prev_tile)

    for p_idx, (window, d) in enumerate(DIL_PATTERNS):
        assert window // d == blk and DIL_TQ % (d * blk) == 0
        n_blk = DIL_TQ // (d * blk)
        for r in range(d):
            for b in range(n_blk):
                def strided(bb):
                    return pl.ds(r + d * blk * bb, blk, stride=d) if d > 1 else pl.ds(blk * bb, blk)
                rows = strided(b)
                if b > 0:
                    kp, vp, mask = kc_ref[0, strided(b - 1), :], vc_ref[0, strided(b - 1), :], band
                else:
                    kp, vp = kp_ref[0, strided(n_blk - 1), :], vp_ref[0, strided(n_blk - 1), :]
                    mask = band_seq_start
                q = q_ref[0, rows, :]
                kcat = jnp.concatenate([kp, kc_ref[0, rows, :]], axis=0).astype(bf16)
                vcat = jnp.concatenate([vp, vc_ref[0, rows, :]], axis=0)
                for h in range(2):
                    qh = jnp.where(head_lanes[h], q, 0.0).astype(bf16)
                    s = jnp.where(mask, _dot_nt(qh, kcat), NEG)
                    mb = jnp.max(s, axis=1, keepdims=True)
                    p = jnp.exp2(s - mb).astype(bf16)
                    vh = jnp.where(head_lanes[h], vcat, 1.0).astype(bf16)
                    pv_ref[p_idx, h, rows, :] = _dot(p, vh)
                    m_ref[p_idx, h, rows, :] = jnp.broadcast_to(mb, (blk, LANES))

    n_pat = len(DIL_PATTERNS)
    chunk = 2 * blk

    def merge(c, carry):
        rows = pl.ds(pl.multiple_of(c * chunk, chunk), chunk)
        outs = []
        for h in range(2):
            ms = [m_ref[p, h, rows, :] for p in range(n_pat)]
            m_all = functools.reduce(jnp.maximum, ms)
            tot = sum(jnp.exp2(ms[p] - m_all) * pv_ref[p, h, rows, :] for p in range(n_pat))
            outs.append(tot / pltpu.roll(tot, HEAD_DIM, 1))
        o_ref[0, rows, :] = jnp.where(head_lanes[0], outs[0], outs[1]).astype(o_ref.dtype)
        return carry

    lax.fori_loop(0, DIL_TQ // chunk, merge, 0)


def _dilated_attention(qd, kd, vd, batch, seq):
    tq = DIL_TQ
    q3, k3, v3 = (a.reshape(batch, seq, DIL_WIDTH) for a in (qd, kd, vd))
    cur = pl.BlockSpec((1, tq, LANES), lambda b, hp, i: (b, i, hp))
    prev = pl.BlockSpec((1, tq, LANES), lambda b, hp, i: (b, jnp.maximum(i - 1, 0), hp))
    out = pl.pallas_call(
        _dilated_kernel,
        grid=(batch, DIL_WIDTH // LANES, seq // tq),
        in_specs=[cur, cur, prev, cur, prev],
        out_specs=cur,
        out_shape=jax.ShapeDtypeStruct((batch, seq, DIL_WIDTH), bf16),
        scratch_shapes=[pltpu.VMEM((len(DIL_PATTERNS), 2, tq, LANES), f32)] * 2,
        compiler_params=pltpu.CompilerParams(dimension_semantics=("parallel", "parallel", "arbitrary"),
                                             vmem_limit_bytes=VMEM_LIMIT),
        name="dilated_attn",
    )(q3, k3, k3, v3, v3)
    return out.reshape(batch * seq, DIL_WIDTH)


def _mla_kernel(q_ref, k_ref, v_ref, o_ref, acc_ref, m_ref):
    tq, tk = MLA_TQ, MLA_TK
    lane = lax.broadcasted_iota(jnp.int32, (1, LANES), 1)

    def query_tile(i, carry):
        q_rows = pl.ds(pl.multiple_of(i * tq, tq), tq)
        m_ref[...] = jnp.full(m_ref.shape, NEG, f32)
        acc_ref[...] = jnp.zeros(acc_ref.shape, f32)

        def step(j, n_tiles, masked):
            nk = n_tiles * tk
            rows = pl.ds(pl.multiple_of(j * tk, tk), nk)
            if masked:
                col = lax.broadcasted_iota(jnp.int32, (tq, nk), 1)
                causal = col <= lax.broadcasted_iota(jnp.int32, (tq, nk), 0) + (nk - tk)
            for h in range(2):
                hs = slice(h * LANES, (h + 1) * LANES)
                s = _dot_nt(q_ref[0, q_rows, hs], k_ref[0, rows, hs])
                if masked:
                    s = jnp.where(causal, s, NEG)
                m_old = m_ref[h]
                m_new = jnp.maximum(m_old, jnp.max(s, axis=1, keepdims=True))
                p = jnp.concatenate([jnp.exp2(s[:, c * LANES:(c + 1) * LANES] - m_new)
                                     for c in range(nk // LANES)], axis=1).astype(bf16)
                acc_ref[h] = acc_ref[h] * jnp.exp2(m_old - m_new) + _dot(p, v_ref[0, rows, hs])
                m_ref[h] = m_new

        def body(j2, c):
            step(2 * j2, 2, False)
            return c

        lax.fori_loop(0, i // 2, body, 0)

        @pl.when(i % 2 == 0)
        def _():
            step(i, 1, True)

        @pl.when(i % 2 == 1)
        def _():
            step(i - 1, 2, True)

        acc0, acc1 = acc_ref[0], acc_ref[1]
        out = jnp.where(lane < V_HEAD, acc0 / pltpu.roll(acc0, V_HEAD, 1), pltpu.roll(acc1, V_HEAD, 1) / acc1)
        o_ref[0, q_rows, :] = out.astype(o_ref.dtype)
        return carry

    lax.fori_loop(0, q_ref.shape[1] // tq, query_tile, 0)


def _mla_attention(qm, km, vm, batch, seq):
    assert MLA_TQ == MLA_TK
    tq = MLA_TQ
    mla_w = N_HEADS_MLA * MLA_HEAD_PAD
    q3 = qm.reshape(batch, seq, mla_w)
    k3 = km.reshape(batch, seq, mla_w)
    v3 = vm.reshape(batch, seq, mla_w)
    out = pl.pallas_call(
        _mla_kernel,
        grid=(batch, N_HEADS_MLA // 2),
        in_specs=[pl.BlockSpec((1, seq, 2 * LANES), lambda b, hp: (b, 0, hp))] * 3,
        out_specs=pl.BlockSpec((1, seq, LANES), lambda b, hp: (b, 0, hp)),
        out_shape=jax.ShapeDtypeStruct((batch, seq, MLA_WIDTH), bf16),
        scratch_shapes=[pltpu.VMEM((2, tq, LANES), f32), pltpu.VMEM((2, tq, LANES), f32)],
        compiler_params=pltpu.CompilerParams(dimension_semantics=("parallel", "parallel"),
                                             vmem_limit_bytes=VMEM_LIMIT),
        name="mla_attn",
    )(q3, k3, v3)
    return out.reshape(batch * seq, MLA_WIDTH)


def _post_attn_kernel(x_ref, od_ref, om_ref, gd_ref, gm_ref, wo_ref, g1_ref, b1_ref, rw_ref,
                      ws1_ref, ws3_ref, ws2_ref, base_ref, hrow_ref, logit_ref):
    tm = POST_TM
    yd = _rms(od_ref[...].astype(f32), gd_ref[...]).astype(bf16)
    ym = _rms(om_ref[...].astype(f32), gm_ref[...]).astype(bf16)
    mix = _dot(yd, wo_ref[0:DIL_WIDTH, :]) + _dot(ym, wo_ref[DIL_WIDTH:, :])
    h1 = _layernorm(DEEPNORM_ALPHA * x_ref[...] + mix, g1_ref[...], b1_ref[...])
    for s in range(ROW_TILE):
        hrow_ref[pl.ds(s, tm, stride=ROW_TILE), :] = h1[:, s * LANES:(s + 1) * LANES]
    hb = h1.astype(bf16)
    logit_ref[...] = _dot_nt(rw_ref[...], hb)
    act = (_silu(_dot(hb, ws1_ref[...])) * _dot(hb, ws3_ref[...])).astype(bf16)
    base_ref[...] = DEEPNORM_ALPHA * h1 + _dot(act, ws2_ref[...])


def _post_attention(x2d, o_dil, o_mla, g_out_dil, g_out_mla, w_o, ln1_g, ln1_b, router_w, ws1, ws3, ws2):
    T = x2d.shape[0]
    tm = POST_TM
    full = lambda a: pl.BlockSpec(a.shape, lambda i: (0,) * a.ndim)
    rows = lambda w: pl.BlockSpec((tm, w), lambda i: (i, 0))
    args = (x2d, o_dil, o_mla, g_out_dil.reshape(1, -1), g_out_mla.reshape(1, -1), w_o.astype(bf16),
            ln1_g.reshape(1, -1), ln1_b.reshape(1, -1), router_w.astype(bf16),
            ws1.astype(bf16), ws3.astype(bf16), ws2.astype(bf16))
    return pl.pallas_call(
        _post_attn_kernel,
        grid=(T // tm,),
        in_specs=[rows(D_MODEL), rows(DIL_WIDTH), rows(MLA_WIDTH)] + [full(a) for a in args[3:]],
        out_specs=[rows(D_MODEL), pl.BlockSpec((tm * ROW_TILE, LANES), lambda i: (i, 0)),
                   pl.BlockSpec((N_EXPERTS, tm), lambda i: (0, i))],
        out_shape=[jax.ShapeDtypeStruct((T, D_MODEL), f32),
                   jax.ShapeDtypeStruct((T * ROW_TILE, LANES), f32),
                   jax.ShapeDtypeStruct((N_EXPERTS, T), f32)],
        compiler_params=pltpu.CompilerParams(dimension_semantics=("parallel",), vmem_limit_bytes=VMEM_LIMIT),
        name="post_attn",
    )(*args)


def _first_argmax(vals, row_ids, n_rows):
    mx = jnp.max(vals, axis=0, keepdims=True)
    idx = jnp.min(jnp.where(vals == mx, row_ids, n_rows), axis=0, keepdims=True)
    return mx, idx


def _router_kernel(logit_ref, bias_ref, e_ref, rank_ref, gw_ref, count_ref, carry_ref):
    tt = ROUTE_TT
    step = pl.program_id(0)

    @pl.when(step == 0)
    def _():
        carry_ref[...] = jnp.zeros_like(carry_ref)

    scores = 1.0 / (1.0 + jnp.exp(-logit_ref[...]))
    choice = scores + bias_ref[...]
    row = lax.broadcasted_iota(jnp.int32, (N_EXPERTS, tt), 0)
    grow = lax.broadcasted_iota(jnp.int32, (GROUP_SIZE, tt), 0)
    neg_inf = -jnp.inf

    gscore = []
    for g in range(N_GROUP):
        cg = choice[g * GROUP_SIZE:(g + 1) * GROUP_SIZE, :]
        m1, i1 = _first_argmax(cg, grow, GROUP_SIZE)
        m2 = jnp.max(jnp.where(grow == i1, neg_inf, cg), axis=0, keepdims=True)
        gscore.append(m1 + m2)
    masked = []
    for g in range(N_GROUP):
        beaten = jnp.zeros((1, tt), jnp.int32)
        for o in range(N_GROUP):
            if o == g:
                continue
            wins = (gscore[o] >= gscore[g]) if o < g else (gscore[o] > gscore[g])
            beaten = beaten + wins.astype(jnp.int32)
        keep = beaten < TOPK_GROUP
        masked.append(jnp.where(keep, choice[g * GROUP_SIZE:(g + 1) * GROUP_SIZE, :], neg_inf))
    cur = jnp.concatenate(masked, axis=0)

    sel_idx, sel_score = [], []
    for _ in range(TOP_K):
        _, idx = _first_argmax(cur, row, N_EXPERTS)
        hit = row == idx
        sel_idx.append(idx)
        sel_score.append(jnp.sum(jnp.where(hit, scores, 0.0), axis=0, keepdims=True))
        cur = jnp.where(hit, neg_inf, cur)
    onehot = jnp.zeros((N_EXPERTS, tt), f32)
    for idx in sel_idx:
        onehot = onehot + (row == idx).astype(f32)

    upper = (lax.broadcasted_iota(jnp.int32, (tt, tt), 0) < lax.broadcasted_iota(jnp.int32, (tt, tt), 1))
    before = _dot(onehot.astype(bf16), upper.astype(bf16)) + carry_ref[...]
    carry_ref[...] = carry_ref[...] + jnp.sum(onehot, axis=1, keepdims=True)
    count_ref[...] = carry_ref[...].astype(jnp.int32)

    denom = sel_score[0]
    for s in sel_score[1:]:
        denom = denom + s
    for k in range(TOP_K):
        hit = row == sel_idx[k]
        e_ref[k:k + 1, :] = sel_idx[k]
        rank_ref[k:k + 1, :] = jnp.sum(jnp.where(hit, before, 0.0), axis=0, keepdims=True).astype(jnp.int32)
        gw_ref[k:k + 1, :] = sel_score[k] / denom * ROUTED_SCALE


def _route(logits_t, router_bias):
    T = logits_t.shape[1]
    tt = ROUTE_TT
    slot = pl.BlockSpec((TOP_K, tt), lambda i: (0, i))
    col = pl.BlockSpec((N_EXPERTS, 1), lambda i: (0, 0))
    return pl.pallas_call(
        _router_kernel,
        grid=(T // tt,),
        in_specs=[pl.BlockSpec((N_EXPERTS, tt), lambda i: (0, i)), col],
        out_specs=[slot, slot, slot, col],
        out_shape=[jax.ShapeDtypeStruct((TOP_K, T), jnp.int32), jax.ShapeDtypeStruct((TOP_K, T), jnp.int32),
                   jax.ShapeDtypeStruct((TOP_K, T), f32), jax.ShapeDtypeStruct((N_EXPERTS, 1), jnp.int32)],
        scratch_shapes=[pltpu.VMEM((N_EXPERTS, 1), f32)],
        compiler_params=pltpu.CompilerParams(dimension_semantics=("arbitrary",), vmem_limit_bytes=VMEM_LIMIT),
        name="router",
    )(logits_t, router_bias.reshape(N_EXPERTS, 1))


def _dest_kernel(e_ref, rank_ref, start_ref, dest_ref):
    tt = DEST_TT
    row = lax.broadcasted_iota(jnp.int32, (N_EXPERTS, tt), 0)
    start = start_ref[...]
    for k in range(TOP_K):
        hit = row == e_ref[k:k + 1, :]
        base = jnp.sum(jnp.where(hit, start, 0.0), axis=0, keepdims=True)
        dest_ref[k:k + 1, :] = base.astype(jnp.int32) + rank_ref[k:k + 1, :]


def _destinations(e_t, rank_t, start):
    T = e_t.shape[1]
    tt = DEST_TT
    assert T * TOP_K < 2 ** 24
    slot = pl.BlockSpec((TOP_K, tt), lambda i: (0, i))
    return pl.pallas_call(
        _dest_kernel,
        grid=(T // tt,),
        in_specs=[slot, slot, pl.BlockSpec((N_EXPERTS, 1), lambda i: (0, 0))],
        out_specs=slot,
        out_shape=jax.ShapeDtypeStruct((TOP_K, T), jnp.int32),
        compiler_params=pltpu.CompilerParams(dimension_semantics=("parallel",)),
        name="destinations",
    )(e_t, rank_t, start.astype(f32).reshape(N_EXPERTS, 1))


def _row_tile(ref, idx):
    return ref.at[pl.ds(pl.multiple_of(idx * ROW_TILE, ROW_TILE), ROW_TILE), :]


def _dispatch_kernel(dest_ref, h_ref, xs_hbm, sem):
    tt = DISPATCH_TT

    def issue(j, carry):
        src = _row_tile(h_ref, j)
        for k in range(TOP_K):
            pltpu.make_async_copy(src, _row_tile(xs_hbm, dest_ref[k, j]), sem).start(priority=k % 2)
        return carry

    lax.fori_loop(0, tt, issue, 0, unroll=ROW_LOOP_UNROLL)
    for _ in range(TOP_K):
        pltpu.make_async_copy(h_ref, xs_hbm.at[pl.ds(0, tt * ROW_TILE), :], sem).wait()


def _dispatch(dest_t, h_rows):
    T = dest_t.shape[1]
    tt = DISPATCH_TT
    slot = pl.BlockSpec((TOP_K, tt), lambda i: (0, i), memory_space=pltpu.SMEM)
    return pl.pallas_call(
        _dispatch_kernel,
        grid=(T // tt,),
        in_specs=[slot, pl.BlockSpec((tt * ROW_TILE, LANES), lambda i: (i, 0))],
        out_specs=pl.BlockSpec(memory_space=pl.ANY),
        scratch_shapes=[pltpu.SemaphoreType.DMA(())],
        out_shape=jax.ShapeDtypeStruct((T * TOP_K * ROW_TILE, LANES), f32),
        compiler_params=pltpu.CompilerParams(dimension_semantics=("arbitrary",)),
        name="dispatch",
    )(dest_t, h_rows)


def _moe_kernel(tile_ref, exp_ref, valid_ref, start_ref, end_ref, x_ref, w1_ref, w3_ref, w2_ref, y_ref,
                w1b_ref, w3b_ref, w2b_ref):
    bm = MOE_BM
    i = pl.program_id(0)
    prev = jnp.maximum(i - 1, 0)
    e, tile = exp_ref[i], tile_ref[i]
    new_expert = (i == 0) | (e != exp_ref[prev])
    new_tile = (i == 0) | (tile != tile_ref[prev])
    row0 = tile * bm
    shared_tile = (start_ref[e] > row0) | (end_ref[e] < row0 + bm)
    valid = valid_ref[i] == 1

    @pl.when(new_expert)
    def _():
        w1b_ref[...] = w1_ref[...].astype(bf16)
        w3b_ref[...] = w3_ref[...].astype(bf16)
        w2b_ref[...] = w2_ref[...].astype(bf16)

    @pl.when(new_tile & shared_tile)
    def _():
        y_ref[...] = jnp.zeros_like(y_ref)

    def expert_rows():
        xb = jnp.concatenate([x_ref[pl.ds(s, bm, stride=ROW_TILE), :].astype(bf16) for s in range(ROW_TILE)],
                             axis=1)
        act = (_silu(_dot(xb, w1b_ref[...])) * _dot(xb, w3b_ref[...])).astype(bf16)
        y = _dot(act, w2b_ref[...])
        return [y[:, s * LANES:(s + 1) * LANES] for s in range(ROW_TILE)]

    @pl.when(valid & jnp.logical_not(shared_tile))
    def _():
        for s, cols in enumerate(expert_rows()):
            y_ref[pl.ds(s, bm, stride=ROW_TILE), :] = cols

    @pl.when(valid & shared_tile)
    def _():
        row = row0 + lax.broadcasted_iota(jnp.int32, (bm, 1), 0)
        mine = (row >= start_ref[e]) & (row < end_ref[e])
        for s, cols in enumerate(expert_rows()):
            rows = pl.ds(s, bm, stride=ROW_TILE)
            y_ref[rows, :] = jnp.where(mine, cols, y_ref[rows, :])


def _work_items(counts, n_rows):
    bm = MOE_BM
    n_tiles = n_rows // bm
    n_items = n_tiles + N_EXPERTS
    end = jnp.cumsum(counts)
    start = end - counts
    first_tile = start // bm
    last_tile = jnp.maximum(end - 1, 0) // bm
    per_expert = jnp.where(counts > 0, last_tile - first_tile + 1, 0)
    item_end = jnp.cumsum(per_expert)
    item_start = item_end - per_expert
    ids = jnp.arange(n_items, dtype=jnp.int32)
    used = item_end[-1]
    valid = ids < used
    ids_c = jnp.minimum(ids, used - 1)
    exp_of = jnp.sum(item_end[None, :] <= ids_c[:, None], axis=1).astype(jnp.int32)
    pick = exp_of[:, None] == jnp.arange(N_EXPERTS, dtype=jnp.int32)[None, :]
    tile_of = jnp.sum(jnp.where(pick, (first_tile - item_start)[None, :], 0), axis=1) + ids_c
    return (tile_of.astype(jnp.int32), exp_of, valid.astype(jnp.int32), start.astype(jnp.int32),
            end.astype(jnp.int32))


def _experts(items, xs, w1, w3, w2):
    bm = MOE_BM
    tile_of, exp_of, valid, start, end = items
    n_items = tile_of.shape[0]
    rows = pl.BlockSpec((bm * ROW_TILE, LANES), lambda i, t, e, *_: (t[i], 0))
    wspec = lambda a: pl.BlockSpec((None, None) + a.shape[2:], lambda i, t, e, *_: (0, e[i], 0, 0))
    return pl.pallas_call(
        _moe_kernel,
        grid_spec=pltpu.PrefetchScalarGridSpec(
            num_scalar_prefetch=5,
            grid=(n_items,),
            in_specs=[rows, wspec(w1), wspec(w3), wspec(w2)],
            out_specs=rows,
            scratch_shapes=[pltpu.VMEM(w1.shape[2:], bf16),
                            pltpu.VMEM(w3.shape[2:], bf16), pltpu.VMEM(w2.shape[2:], bf16)],
        ),
        out_shape=jax.ShapeDtypeStruct(xs.shape, f32),
        compiler_params=pltpu.CompilerParams(dimension_semantics=("arbitrary",), vmem_limit_bytes=VMEM_LIMIT),
        name="experts",
    )(tile_of, exp_of, valid, start, end, xs, w1, w3, w2)


def _combine_kernel(dest_ref, gw_ref, dest_next_ref, y_hbm, base_ref, g2_ref, b2_ref, o_ref,
                    buf_ref, acc_ref, sem):
    tt = COMBINE_TT
    i = pl.program_id(0)
    cur = i % 2

    def gather(d_ref, slot):
        def issue(j, carry):
            for k in range(TOP_K):
                pltpu.make_async_copy(_row_tile(y_hbm, d_ref[k, j]), _row_tile(buf_ref.at[slot, k], j),
                                      sem.at[slot]).start(priority=k % 2)
            return carry

        lax.fori_loop(0, tt, issue, 0, unroll=ROW_LOOP_UNROLL)

    @pl.when(i == 0)
    def _():
        gather(dest_ref, 0)

    @pl.when(i + 1 < pl.num_programs(0))
    def _():
        gather(dest_next_ref, 1 - cur)

    for k in range(TOP_K):
        pltpu.make_async_copy(y_hbm.at[pl.ds(0, tt * ROW_TILE), :], buf_ref.at[cur, k], sem.at[cur]).wait()

    def weigh(j, carry):
        rows = pl.ds(pl.multiple_of(j * ROW_TILE, ROW_TILE), ROW_TILE)
        tot = gw_ref[0, j] * buf_ref[cur, 0, rows, :]
        for k in range(1, TOP_K):
            tot = tot + gw_ref[k, j] * buf_ref[cur, k, rows, :]
        acc_ref[rows, :] = tot
        return carry

    lax.fori_loop(0, tt, weigh, 0, unroll=ROW_LOOP_UNROLL)
    routed = jnp.concatenate([acc_ref[pl.ds(s, tt, stride=ROW_TILE), :] for s in range(ROW_TILE)], axis=1)
    o_ref[...] = _layernorm(base_ref[...] + routed, g2_ref[...], b2_ref[...])


def _combine(dest_t, gw_t, y_rows, base, ln2_g, ln2_b):
    T = base.shape[0]
    tt = COMBINE_TT
    n_steps = T // tt
    slot = pl.BlockSpec((TOP_K, tt), lambda i: (0, i), memory_space=pltpu.SMEM)
    slot_next = pl.BlockSpec((TOP_K, tt), lambda i: (0, jnp.minimum(i + 1, n_steps - 1)), memory_space=pltpu.SMEM)
    vec = pl.BlockSpec((1, D_MODEL), lambda i: (0, 0))
    rows = pl.BlockSpec((tt, D_MODEL), lambda i: (i, 0))
    return pl.pallas_call(
        _combine_kernel,
        grid=(n_steps,),
        in_specs=[slot, slot, slot_next, pl.BlockSpec(memory_space=pl.ANY), rows, vec, vec],
        out_specs=rows,
        scratch_shapes=[pltpu.VMEM((2, TOP_K, tt * ROW_TILE, LANES), f32), pltpu.VMEM((tt * ROW_TILE, LANES), f32),
                        pltpu.SemaphoreType.DMA((2,))],
        out_shape=jax.ShapeDtypeStruct((T, D_MODEL), f32),
        compiler_params=pltpu.CompilerParams(dimension_semantics=("arbitrary",), vmem_limit_bytes=VMEM_LIMIT),
        name="combine",
    )(dest_t, gw_t, dest_t, y_rows, base, ln2_g.reshape(1, -1), ln2_b.reshape(1, -1))


def _layer(x, w_in, g_q, w_uq, g_kv, w_ukv, g_out_dil, g_out_mla, w_o, ln1_g, ln1_b,
           router_w, router_bias, w1, w3, w2, ws1, ws3, ws2, ln2_g, ln2_b, layer):
    batch, seq, _ = x.shape
    T = batch * seq
    x2d = x.reshape(T, D_MODEL)
    qd, kd, vd, qm, km, vm = _project(x2d, w_in[layer], w_uq[layer], w_ukv[layer], g_q[layer], g_kv[layer], seq)
    o_dil = _dilated_attention(qd, kd, vd, batch, seq)
    o_mla = _mla_attention(qm, km, vm, batch, seq)
    base, h_rows, logits_t = _post_attention(
        x2d, o_dil, o_mla, g_out_dil[layer], g_out_mla[layer], w_o[layer], ln1_g[layer], ln1_b[layer],
        router_w[layer], ws1[layer], ws3[layer], ws2[layer])
    e_t, rank_t, gw_t, counts = _route(logits_t, router_bias[layer])
    items = _work_items(counts.reshape(N_EXPERTS), T * TOP_K)
    dest_t = _destinations(e_t, rank_t, items[3])
    xs = _dispatch(dest_t, h_rows)
    y_rows = _experts(items, xs, w1[layer:layer + 1], w3[layer:layer + 1], w2[layer:layer + 1])
    out = _combine(dest_t, gw_t, y_rows, base, ln2_g[layer], ln2_b[layer])
    return out.reshape(batch, seq, D_MODEL)


def kernel(x, w_in, g_q, w_uq, g_kv, w_ukv, g_out_dil, g_out_mla, w_o, ln1_g, ln1_b, router_w, router_bias,
           w1, w3, w2, ws1, ws3, ws2, ln2_g, ln2_b):
    assert x.shape[1] % DIL_TQ == 0 and x.shape[2] == D_MODEL
    h = x
    for layer in range(w_in.shape[0]):
        h = _layer(h, w_in, g_q, w_uq, g_kv, w_ukv, g_out_dil, g_out_mla, w_o, ln1_g, ln1_b,
                   router_w, router_bias, w1, w3, w2, ws1, ws3, ws2, ln2_g, ln2_b, layer)
    return h
```

```python
import functools

import jax
import jax.numpy as jnp
from jax import lax
from jax.experimental import pallas as pl
from jax.experimental.pallas import tpu as pltpu

D_MODEL = 1024
HEAD_DIM = 64
N_HEADS_DIL = 8
DIL_PATTERNS = ((128, 1), (512, 4), (2048, 16))
ROPE_DIM_DIL = HEAD_DIM // 4
ROPE_THETA = 500000.0
ATTN_BLOCK = 128
N_HEADS_MLA = 8
Q_LORA = 384
KV_LORA = 128
QK_NOPE = 64
QK_ROPE = 32
V_HEAD = 64
DIL_WIDTH = N_HEADS_DIL * HEAD_DIM
MLA_WIDTH = N_HEADS_MLA * V_HEAD
N_EXPERTS = 256
TOP_K = 8
N_GROUP = 8
TOPK_GROUP = 4
GROUP_SIZE = N_EXPERTS // N_GROUP
EXPERT_FF = 256
SHARED_FF = 256
ROUTED_SCALE = 2.5
DEPTH = 1
DEEPNORM_ALPHA = (2.0 * DEPTH) ** 0.25
LN_EPS = 1e-5
RMS_EPS = 1e-6
NEG = -1e30
LOG2E = 1.4426950408889634

LANES = 128
SUBLANES = 8
VMEM_LIMIT = 56 * 1024 * 1024

PROJ_TM = 512
DIL_TQ = 2048
MLA_TQ = 512
MLA_TK = 512
POST_TM = 512
ROUTE_TT = 512
DEST_TT = 2048
DISPATCH_TT = 1024
MOE_BM = 1024
COMBINE_TT = 512
ROW_LOOP_UNROLL = 4

MLA_HEAD_PAD = LANES

bf16 = jnp.bfloat16
f32 = jnp.float32


def _dot(a, b):
    return jnp.dot(a, b, preferred_element_type=f32)


def _dot_nt(a, b):
    return lax.dot_general(a, b, (((1,), (1,)), ((), ())), preferred_element_type=f32)


def _rope_lanes(x, tab_ref, shift):
    return (x * tab_ref[0]
            + pltpu.roll(x, shift, 1) * tab_ref[1]
            + pltpu.roll(x, LANES - shift, 1) * tab_ref[2])


def _rms(x, g):
    return x * lax.rsqrt(jnp.mean(x * x, axis=-1, keepdims=True) + RMS_EPS) * g


def _layernorm(x, g, b):
    mu = jnp.mean(x, axis=-1, keepdims=True)
    xc = x - mu
    var = jnp.mean(xc * xc, axis=-1, keepdims=True)
    return xc * lax.rsqrt(var + LN_EPS) * g + b


def _silu(x):
    return x * (1.0 / (1.0 + jnp.exp(-x)))


ROW_TILE = D_MODEL // LANES
assert ROW_TILE == SUBLANES


def _proj_kernel(x_ref, win_ref, wuq_ref, wukv_ref, gq_ref, gkv_ref, ropd_ref, ropm_ref,
                 qd_ref, kd_ref, vd_ref, qm_ref, km_ref, vm_ref):
    xb = x_ref[...].astype(bf16)
    o_k, o_v, o_cq, o_ckv, o_kpe = DIL_WIDTH, 2 * DIL_WIDTH, 3 * DIL_WIDTH, 3 * DIL_WIDTH + Q_LORA, \
        3 * DIL_WIDTH + Q_LORA + KV_LORA
    rope_shift_d = ROPE_DIM_DIL // 2
    rope_shift_m = QK_ROPE // 2

    q = _dot(xb, win_ref[:, 0:o_k])
    k = _dot(xb, win_ref[:, o_k:o_v])
    for j in range(DIL_WIDTH // LANES):
        sl = slice(j * LANES, (j + 1) * LANES)
        qd_ref[:, sl] = _rope_lanes(q[:, sl], ropd_ref, rope_shift_d) * (HEAD_DIM ** -0.5 * LOG2E)
        kd_ref[:, sl] = _rope_lanes(k[:, sl], ropd_ref, rope_shift_d)
    vd_ref[...] = _dot(xb, win_ref[:, o_v:o_cq])

    cq = _rms(_dot(xb, win_ref[:, o_cq:o_ckv]), gq_ref[...]).astype(bf16)
    qm = _dot(cq, wuq_ref[...])
    ckv = _rms(_dot(xb, win_ref[:, o_ckv:o_kpe]), gkv_ref[...]).astype(bf16)
    kn = _dot(ckv, wukv_ref[:, 0:N_HEADS_MLA * MLA_HEAD_PAD])
    kpe = _rope_lanes(_dot(xb, win_ref[:, o_kpe:o_kpe + LANES]), ropm_ref, rope_shift_m)
    vv = _dot(ckv, wukv_ref[:, N_HEADS_MLA * MLA_HEAD_PAD:])
    scale = (QK_NOPE + QK_ROPE) ** -0.5 * LOG2E
    ones_pad = (lax.broadcasted_iota(jnp.int32, (1, LANES), 1) >= V_HEAD).astype(f32)
    for h in range(N_HEADS_MLA):
        sl = slice(h * LANES, (h + 1) * LANES)
        qm_ref[:, sl] = (_rope_lanes(qm[:, sl], ropm_ref, rope_shift_m) * scale).astype(bf16)
        km_ref[:, sl] = (kn[:, sl] + kpe).astype(bf16)
        vm_ref[:, sl] = (vv[:, sl] + ones_pad).astype(bf16)


def _rope_tables(seq):
    pos = jnp.arange(seq, dtype=f32)[:, None]
    lane = jnp.arange(LANES)

    def build(dim, lane_in_head, period):
        half = dim // 2
        inv = ROPE_THETA ** (-jnp.arange(0, dim, 2, dtype=f32) / dim)
        ang = pos * inv[None, :]
        cos, sin = jnp.cos(ang), jnp.sin(ang)
        off = lane_in_head(lane % period)
        in_x1 = (off >= 0) & (off < half)
        in_x2 = (off >= half) & (off < dim)
        idx = jnp.clip(jnp.where(in_x2, off - half, off), 0, half - 1)
        c = jnp.where((in_x1 | in_x2)[None, :], cos[:, idx], 1.0)
        sa = jnp.where(in_x2[None, :], sin[:, idx], 0.0)
        sb = jnp.where(in_x1[None, :], -sin[:, idx], 0.0)
        return jnp.stack([c, sa, sb]).astype(f32)

    tab_d = build(ROPE_DIM_DIL, lambda l: jnp.where(l < ROPE_DIM_DIL, l, -1), HEAD_DIM)
    tab_m = build(QK_ROPE, lambda l: jnp.where((l >= QK_NOPE) & (l < QK_NOPE + QK_ROPE), l - QK_NOPE, -1), LANES)
    return tab_d, tab_m


def _project(x2d, w_in, w_uq, w_ukv, g_q, g_kv, seq):
    T = x2d.shape[0]
    tm = PROJ_TM
    n_seq_tiles = seq // tm
    tab_d, tab_m = _rope_tables(seq)

    kpe_cols = jnp.zeros((D_MODEL, LANES), f32).at[:, QK_NOPE:QK_NOPE + QK_ROPE].set(w_in[:, -QK_ROPE:])
    w_in_b = jnp.concatenate([w_in[:, :-QK_ROPE], kpe_cols], axis=1).astype(bf16)
    w_uq_b = jnp.pad(w_uq.reshape(Q_LORA, N_HEADS_MLA, QK_NOPE + QK_ROPE),
                     ((0, 0), (0, 0), (0, MLA_HEAD_PAD - QK_NOPE - QK_ROPE))
                     ).reshape(Q_LORA, N_HEADS_MLA * MLA_HEAD_PAD).astype(bf16)
    w_ukv3 = w_ukv.reshape(KV_LORA, N_HEADS_MLA, QK_NOPE + V_HEAD)
    w_uk = jnp.pad(w_ukv3[:, :, :QK_NOPE], ((0, 0), (0, 0), (0, MLA_HEAD_PAD - QK_NOPE))
                   ).reshape(KV_LORA, N_HEADS_MLA * MLA_HEAD_PAD)
    w_uv = jnp.pad(w_ukv3[:, :, QK_NOPE:], ((0, 0), (0, 0), (0, MLA_HEAD_PAD - V_HEAD))
                   ).reshape(KV_LORA, N_HEADS_MLA * MLA_HEAD_PAD)
    w_ukv_b = jnp.concatenate([w_uk, w_uv], axis=1).astype(bf16)

    full = lambda a: pl.BlockSpec(a.shape, lambda i: (0,) * a.ndim)
    rows = lambda w: pl.BlockSpec((tm, w), lambda i: (i, 0))
    tab_spec = pl.BlockSpec((3, tm, LANES), lambda i: (0, i % n_seq_tiles, 0))
    gq2, gkv2 = g_q.reshape(1, Q_LORA), g_kv.reshape(1, KV_LORA)
    mla_w = N_HEADS_MLA * MLA_HEAD_PAD
    return pl.pallas_call(
        _proj_kernel,
        grid=(T // tm,),
        in_specs=[rows(D_MODEL), full(w_in_b), full(w_uq_b), full(w_ukv_b), full(gq2), full(gkv2),
                  tab_spec, tab_spec],
        out_specs=[rows(DIL_WIDTH), rows(DIL_WIDTH), rows(DIL_WIDTH), rows(mla_w), rows(mla_w), rows(mla_w)],
        out_shape=[jax.ShapeDtypeStruct((T, DIL_WIDTH), f32)] * 3 + [jax.ShapeDtypeStruct((T, mla_w), bf16)] * 3,
        compiler_params=pltpu.CompilerParams(dimension_semantics=("parallel",), vmem_limit_bytes=VMEM_LIMIT),
        name="proj",
    )(x2d, w_in_b, w_uq_b, w_ukv_b, gq2, gkv2, tab_d, tab_m)


def _dilated_kernel(q_ref, kc_ref, kp_ref, vc_ref, vp_ref, o_ref, pv_ref, m_ref):
    blk = ATTN_BLOCK
    has_prev_tile = pl.program_id(2) > 0
    lane = lax.broadcasted_iota(jnp.int32, (1, LANES), 1)
    head_lanes = (lane < HEAD_DIM, lane >= HEAD_DIM)
    qi = lax.broadcasted_iota(jnp.int32, (blk, 2 * blk), 0)
    ki = lax.broadcasted_iota(jnp.int32, (blk, 2 * blk), 1)
    band = (ki >= qi) & (ki <= qi + blk)
    band_seq_start = band & ((ki >= blk) | has_prev_tile)

    for p_idx, (window, d) in enumerate(DIL_PATTERNS):
        assert window // d == blk and DIL_TQ % (d * blk) == 0
        n_blk = DIL_TQ // (d * blk)
        for r in range(d):
            for b in range(n_blk):
                def strided(bb):
                    return pl.ds(r + d * blk * bb, blk, stride=d) if d > 1 else pl.ds(blk * bb, blk)
                rows = strided(b)
                if b > 0:
                    kp, vp, mask = kc_ref[0, strided(b - 1), :], vc_ref[0, strided(b - 1), :], band
                else:
                    kp, vp = kp_ref[0, strided(n_blk - 1), :], vp_ref[0, strided(n_blk - 1), :]
                    mask = band_seq_start
                q = q_ref[0, rows, :]
                kcat = jnp.concatenate([kp, kc_ref[0, rows, :]], axis=0).astype(bf16)
                vcat = jnp.concatenate([vp, vc_ref[0, rows, :]], axis=0)
                for h in range(2):
                    qh = jnp.where(head_lanes[h], q, 0.0).astype(bf16)
                    s = jnp.where(mask, _dot_nt(qh, kcat), NEG)
                    mb = jnp.max(s, axis=1, keepdims=True)
                    p = jnp.exp2(s - mb).astype(bf16)
                    vh = jnp.where(head_lanes[h], vcat, 1.0).astype(bf16)
                    pv_ref[p_idx, h, rows, :] = _dot(p, vh)
                    m_ref[p_idx, h, rows, :] = jnp.broadcast_to(mb, (blk, LANES))

    n_pat = len(DIL_PATTERNS)
    chunk = 2 * blk

    def merge(c, carry):
        rows = pl.ds(pl.multiple_of(c * chunk, chunk), chunk)
        outs = []
        for h in range(2):
            ms = [m_ref[p, h, rows, :] for p in range(n_pat)]
            m_all = functools.reduce(jnp.maximum, ms)
            tot = sum(jnp.exp2(ms[p] - m_all) * pv_ref[p, h, rows, :] for p in range(n_pat))
            outs.append(tot / pltpu.roll(tot, HEAD_DIM, 1))
        o_ref[0, rows, :] = jnp.where(head_lanes[0], outs[0], outs[1]).astype(o_ref.dtype)
        return carry

    lax.fori_loop(0, DIL_TQ // chunk, merge, 0)


def _dilated_attention(qd, kd, vd, batch, seq):
    tq = DIL_TQ
    q3, k3, v3 = (a.reshape(batch, seq, DIL_WIDTH) for a in (qd, kd, vd))
    cur = pl.BlockSpec((1, tq, LANES), lambda b, hp, i: (b, i, hp))
    prev = pl.BlockSpec((1, tq, LANES), lambda b, hp, i: (b, jnp.maximum(i - 1, 0), hp))
    out = pl.pallas_call(
        _dilated_kernel,
        grid=(batch, DIL_WIDTH // LANES, seq // tq),
        in_specs=[cur, cur, prev, cur, prev],
        out_specs=cur,
        out_shape=jax.ShapeDtypeStruct((batch, seq, DIL_WIDTH), bf16),
        scratch_shapes=[pltpu.VMEM((len(DIL_PATTERNS), 2, tq, LANES), f32)] * 2,
        compiler_params=pltpu.CompilerParams(dimension_semantics=("parallel", "parallel", "arbitrary"),
                                             vmem_limit_bytes=VMEM_LIMIT),
        name="dilated_attn",
    )(q3, k3, k3, v3, v3)
    return out.reshape(batch * seq, DIL_WIDTH)


def _mla_kernel(q_ref, k_ref, v_ref, o_ref, acc_ref, m_ref):
    tq, tk = MLA_TQ, MLA_TK
    lane = lax.broadcasted_iota(jnp.int32, (1, LANES), 1)

    def query_tile(i, carry):
        q_rows = pl.ds(pl.multiple_of(i * tq, tq), tq)
        m_ref[...] = jnp.full(m_ref.shape, NEG, f32)
        acc_ref[...] = jnp.zeros(acc_ref.shape, f32)

        def step(j, n_tiles, masked):
            nk = n_tiles * tk
            rows = pl.ds(pl.multiple_of(j * tk, tk), nk)
            if masked:
                col = lax.broadcasted_iota(jnp.int32, (tq, nk), 1)
                causal = col <= lax.broadcasted_iota(jnp.int32, (tq, nk), 0) + (nk - tk)
            for h in range(2):
                hs = slice(h * LANES, (h + 1) * LANES)
                s = _dot_nt(q_ref[0, q_rows, hs], k_ref[0, rows, hs])
                if masked:
                    s = jnp.where(causal, s, NEG)
                m_old = m_ref[h]
                m_new = jnp.maximum(m_old, jnp.max(s, axis=1, keepdims=True))
                p = jnp.concatenate([jnp.exp2(s[:, c * LANES:(c + 1) * LANES] - m_new)
                                     for c in range(nk // LANES)], axis=1).astype(bf16)
                acc_ref[h] = acc_ref[h] * jnp.exp2(m_old - m_new) + _dot(p, v_ref[0, rows, hs])
                m_ref[h] = m_new

        def body(j2, c):
            step(2 * j2, 2, False)
            return c

        lax.fori_loop(0, i // 2, body, 0)

        @pl.when(i % 2 == 0)
        def _():
            step(i, 1, True)

        @pl.when(i % 2 == 1)
        def _():
            step(i - 1, 2, True)

        acc0, acc1 = acc_ref[0], acc_ref[1]
        out = jnp.where(lane < V_HEAD, acc0 / pltpu.roll(acc0, V_HEAD, 1), pltpu.roll(acc1, V_HEAD, 1) / acc1)
        o_ref[0, q_rows, :] = out.astype(o_ref.dtype)
        return carry

    lax.fori_loop(0, q_ref.shape[1] // tq, query_tile, 0)


def _mla_attention(qm, km, vm, batch, seq):
    assert MLA_TQ == MLA_TK
    tq = MLA_TQ
    mla_w = N_HEADS_MLA * MLA_HEAD_PAD
    q3 = qm.reshape(batch, seq, mla_w)
    k3 = km.reshape(batch, seq, mla_w)
    v3 = vm.reshape(batch, seq, mla_w)
    out = pl.pallas_call(
        _mla_kernel,
        grid=(batch, N_HEADS_MLA // 2),
        in_specs=[pl.BlockSpec((1, seq, 2 * LANES), lambda b, hp: (b, 0, hp))] * 3,
        out_specs=pl.BlockSpec((1, seq, LANES), lambda b, hp: (b, 0, hp)),
        out_shape=jax.ShapeDtypeStruct((batch, seq, MLA_WIDTH), bf16),
        scratch_shapes=[pltpu.VMEM((2, tq, LANES), f32), pltpu.VMEM((2, tq, LANES), f32)],
        compiler_params=pltpu.CompilerParams(dimension_semantics=("parallel", "parallel"),
                                             vmem_limit_bytes=VMEM_LIMIT),
        name="mla_attn",
    )(q3, k3, v3)
    return out.reshape(batch * seq, MLA_WIDTH)


def _post_attn_kernel(x_ref, od_ref, om_ref, gd_ref, gm_ref, wo_ref, g1_ref, b1_ref, rw_ref,
                      ws1_ref, ws3_ref, ws2_ref, bias_ref, base_ref, hrow_ref, e_ref, rank_ref, gw_ref, count_ref,
                      logit_ref, carry_ref):
    tm = POST_TM
    yd = _rms(od_ref[...].astype(f32), gd_ref[...]).astype(bf16)
    ym = _rms(om_ref[...].astype(f32), gm_ref[...]).astype(bf16)
    mix = _dot(yd, wo_ref[0:DIL_WIDTH, :]) + _dot(ym, wo_ref[DIL_WIDTH:, :])
    h1 = _layernorm(DEEPNORM_ALPHA * x_ref[...] + mix, g1_ref[...], b1_ref[...])
    for s in range(ROW_TILE):
        hrow_ref[pl.ds(s, tm, stride=ROW_TILE), :] = h1[:, s * LANES:(s + 1) * LANES]
    hb = h1.astype(bf16)
    logit_ref[...] = _dot_nt(rw_ref[...], hb)
    act = (_silu(_dot(hb, ws1_ref[...])) * _dot(hb, ws3_ref[...])).astype(bf16)
    base_ref[...] = DEEPNORM_ALPHA * h1 + _dot(act, ws2_ref[...])
    _router_kernel(logit_ref, bias_ref, e_ref, rank_ref, gw_ref, count_ref, carry_ref)


def _post_attention(x2d, o_dil, o_mla, g_out_dil, g_out_mla, w_o, ln1_g, ln1_b, router_w, ws1, ws3, ws2,
                    router_bias):
    T = x2d.shape[0]
    tm = POST_TM
    full = lambda a: pl.BlockSpec(a.shape, lambda i: (0,) * a.ndim)
    rows = lambda w: pl.BlockSpec((tm, w), lambda i: (i, 0))
    args = (x2d, o_dil, o_mla, g_out_dil.reshape(1, -1), g_out_mla.reshape(1, -1), w_o.astype(bf16),
            ln1_g.reshape(1, -1), ln1_b.reshape(1, -1), router_w.astype(bf16),
            ws1.astype(bf16), ws3.astype(bf16), ws2.astype(bf16), router_bias.reshape(N_EXPERTS, 1))
    assert POST_TM == ROUTE_TT
    slot = pl.BlockSpec((TOP_K, tm), lambda i: (0, i))
    return pl.pallas_call(
        _post_attn_kernel,
        grid=(T // tm,),
        in_specs=[rows(D_MODEL), rows(DIL_WIDTH), rows(MLA_WIDTH)] + [full(a) for a in args[3:]],
        out_specs=[rows(D_MODEL), pl.BlockSpec((tm * ROW_TILE, LANES), lambda i: (i, 0)), slot, slot, slot,
                   pl.BlockSpec((N_EXPERTS, 1), lambda i: (0, 0))],
        out_shape=[jax.ShapeDtypeStruct((T, D_MODEL), f32),
                   jax.ShapeDtypeStruct((T * ROW_TILE, LANES), f32),
                   jax.ShapeDtypeStruct((TOP_K, T), jnp.int32), jax.ShapeDtypeStruct((TOP_K, T), jnp.int32),
                   jax.ShapeDtypeStruct((TOP_K, T), f32), jax.ShapeDtypeStruct((N_EXPERTS, 1), jnp.int32)],
        scratch_shapes=[pltpu.VMEM((N_EXPERTS, tm), f32), pltpu.VMEM((N_EXPERTS, 1), f32)],
        compiler_params=pltpu.CompilerParams(dimension_semantics=("arbitrary",), vmem_limit_bytes=VMEM_LIMIT),
        name="post_attn_route",
    )(*args)


def _first_argmax(vals, row_ids, n_rows):
    mx = jnp.max(vals, axis=0, keepdims=True)
    idx = jnp.min(jnp.where(vals == mx, row_ids, n_rows), axis=0, keepdims=True)
    return mx, idx


def _router_kernel(logit_ref, bias_ref, e_ref, rank_ref, gw_ref, count_ref, carry_ref):
    tt = ROUTE_TT
    step = pl.program_id(0)

    @pl.when(step == 0)
    def _():
        carry_ref[...] = jnp.zeros_like(carry_ref)

    scores = 1.0 / (1.0 + jnp.exp(-logit_ref[...]))
    choice = scores + bias_ref[...]
    row = lax.broadcasted_iota(jnp.int32, (N_EXPERTS, tt), 0)
    grow = lax.broadcasted_iota(jnp.int32, (GROUP_SIZE, tt), 0)
    neg_inf = -jnp.inf

    gscore = []
    for g in range(N_GROUP):
        cg = choice[g * GROUP_SIZE:(g + 1) * GROUP_SIZE, :]
        m1, i1 = _first_argmax(cg, grow, GROUP_SIZE)
        m2 = jnp.max(jnp.where(grow == i1, neg_inf, cg), axis=0, keepdims=True)
        gscore.append(m1 + m2)
    masked = []
    for g in range(N_GROUP):
        beaten = jnp.zeros((1, tt), jnp.int32)
        for o in range(N_GROUP):
            if o == g:
                continue
            wins = (gscore[o] >= gscore[g]) if o < g else (gscore[o] > gscore[g])
            beaten = beaten + wins.astype(jnp.int32)
        keep = beaten < TOPK_GROUP
        masked.append(jnp.where(keep, choice[g * GROUP_SIZE:(g + 1) * GROUP_SIZE, :], neg_inf))
    cur = jnp.concatenate(masked, axis=0)

    sel_idx, sel_score = [], []
    for _ in range(TOP_K):
        _, idx = _first_argmax(cur, row, N_EXPERTS)
        hit = row == idx
        sel_idx.append(idx)
        sel_score.append(jnp.sum(jnp.where(hit, scores, 0.0), axis=0, keepdims=True))
        cur = jnp.where(hit, neg_inf, cur)
    onehot = jnp.zeros((N_EXPERTS, tt), f32)
    for idx in sel_idx:
        onehot = onehot + (row == idx).astype(f32)

    upper = (lax.broadcasted_iota(jnp.int32, (tt, tt), 0) < lax.broadcasted_iota(jnp.int32, (tt, tt), 1))
    before = _dot(onehot.astype(bf16), upper.astype(bf16)) + carry_ref[...]
    carry_ref[...] = carry_ref[...] + jnp.sum(onehot, axis=1, keepdims=True)
    count_ref[...] = carry_ref[...].astype(jnp.int32)

    denom = sel_score[0]
    for s in sel_score[1:]:
        denom = denom + s
    for k in range(TOP_K):
        hit = row == sel_idx[k]
        e_ref[k:k + 1, :] = sel_idx[k]
        rank_ref[k:k + 1, :] = jnp.sum(jnp.where(hit, before, 0.0), axis=0, keepdims=True).astype(jnp.int32)
        gw_ref[k:k + 1, :] = sel_score[k] / denom * ROUTED_SCALE


def _dest_kernel(e_ref, rank_ref, start_ref, dest_ref):
    tt = DEST_TT
    row = lax.broadcasted_iota(jnp.int32, (N_EXPERTS, tt), 0)
    start = start_ref[...]
    for k in range(TOP_K):
        hit = row == e_ref[k:k + 1, :]
        base = jnp.sum(jnp.where(hit, start, 0.0), axis=0, keepdims=True)
        dest_ref[k:k + 1, :] = base.astype(jnp.int32) + rank_ref[k:k + 1, :]


def _destinations(e_t, rank_t, start):
    T = e_t.shape[1]
    tt = DEST_TT
    assert T * TOP_K < 2 ** 24
    slot = pl.BlockSpec((TOP_K, tt), lambda i: (0, i))
    return pl.pallas_call(
        _dest_kernel,
        grid=(T // tt,),
        in_specs=[slot, slot, pl.BlockSpec((N_EXPERTS, 1), lambda i: (0, 0))],
        out_specs=slot,
        out_shape=jax.ShapeDtypeStruct((TOP_K, T), jnp.int32),
        compiler_params=pltpu.CompilerParams(dimension_semantics=("parallel",)),
        name="destinations",
    )(e_t, rank_t, start.astype(f32).reshape(N_EXPERTS, 1))


def _row_tile(ref, idx):
    return ref.at[pl.ds(pl.multiple_of(idx * ROW_TILE, ROW_TILE), ROW_TILE), :]


def _dispatch_kernel(dest_ref, h_ref, xs_hbm, sem):
    tt = DISPATCH_TT

    def issue(j, carry):
        src = _row_tile(h_ref, j)
        for k in range(TOP_K):
            pltpu.make_async_copy(src, _row_tile(xs_hbm, dest_ref[k, j]), sem).start(priority=k % 2)
        return carry

    lax.fori_loop(0, tt, issue, 0, unroll=ROW_LOOP_UNROLL)
    for _ in range(TOP_K):
        pltpu.make_async_copy(h_ref, xs_hbm.at[pl.ds(0, tt * ROW_TILE), :], sem).wait()


def _dispatch(dest_t, h_rows):
    T = dest_t.shape[1]
    tt = DISPATCH_TT
    slot = pl.BlockSpec((TOP_K, tt), lambda i: (0, i), memory_space=pltpu.SMEM)
    return pl.pallas_call(
        _dispatch_kernel,
        grid=(T // tt,),
        in_specs=[slot, pl.BlockSpec((tt * ROW_TILE, LANES), lambda i: (i, 0))],
        out_specs=pl.BlockSpec(memory_space=pl.ANY),
        scratch_shapes=[pltpu.SemaphoreType.DMA(())],
        out_shape=jax.ShapeDtypeStruct((T * TOP_K * ROW_TILE, LANES), f32),
        compiler_params=pltpu.CompilerParams(dimension_semantics=("arbitrary",)),
        name="dispatch",
    )(dest_t, h_rows)


def _moe_kernel(tile_ref, exp_ref, valid_ref, start_ref, end_ref, x_ref, w1_ref, w3_ref, w2_ref, y_ref,
                w1b_ref, w3b_ref, w2b_ref):
    bm = MOE_BM
    i = pl.program_id(0)
    prev = jnp.maximum(i - 1, 0)
    e, tile = exp_ref[i], tile_ref[i]
    new_expert = (i == 0) | (e != exp_ref[prev])
    new_tile = (i == 0) | (tile != tile_ref[prev])
    row0 = tile * bm
    shared_tile = (start_ref[e] > row0) | (end_ref[e] < row0 + bm)
    valid = valid_ref[i] == 1

    @pl.when(new_expert)
    def _():
        w1b_ref[...] = w1_ref[...].astype(bf16)
        w3b_ref[...] = w3_ref[...].astype(bf16)
        w2b_ref[...] = w2_ref[...].astype(bf16)

    @pl.when(new_tile & shared_tile)
    def _():
        y_ref[...] = jnp.zeros_like(y_ref)

    def expert_rows():
        xb = jnp.concatenate([x_ref[pl.ds(s, bm, stride=ROW_TILE), :].astype(bf16) for s in range(ROW_TILE)],
                             axis=1)
        act = (_silu(_dot(xb, w1b_ref[...])) * _dot(xb, w3b_ref[...])).astype(bf16)
        y = _dot(act, w2b_ref[...])
        return [y[:, s * LANES:(s + 1) * LANES] for s in range(ROW_TILE)]

    @pl.when(valid & jnp.logical_not(shared_tile))
    def _():
        for s, cols in enumerate(expert_rows()):
            y_ref[pl.ds(s, bm, stride=ROW_TILE), :] = cols

    @pl.when(valid & shared_tile)
    def _():
        row = row0 + lax.broadcasted_iota(jnp.int32, (bm, 1), 0)
        mine = (row >= start_ref[e]) & (row < end_ref[e])
        for s, cols in enumerate(expert_rows()):
            rows = pl.ds(s, bm, stride=ROW_TILE)
            y_ref[rows, :] = jnp.where(mine, cols, y_ref[rows, :])


def _work_items(counts, n_rows):
    bm = MOE_BM
    n_tiles = n_rows // bm
    n_items = n_tiles + N_EXPERTS
    end = jnp.cumsum(counts)
    start = end - counts
    first_tile = start // bm
    last_tile = jnp.maximum(end - 1, 0) // bm
    per_expert = jnp.where(counts > 0, last_tile - first_tile + 1, 0)
    item_end = jnp.cumsum(per_expert)
    item_start = item_end - per_expert
    ids = jnp.arange(n_items, dtype=jnp.int32)
    used = item_end[-1]
    valid = ids < used
    ids_c = jnp.minimum(ids, used - 1)
    exp_of = jnp.sum(item_end[None, :] <= ids_c[:, None], axis=1).astype(jnp.int32)
    pick = exp_of[:, None] == jnp.arange(N_EXPERTS, dtype=jnp.int32)[None, :]
    tile_of = jnp.sum(jnp.where(pick, (first_tile - item_start)[None, :], 0), axis=1) + ids_c
    return (tile_of.astype(jnp.int32), exp_of, valid.astype(jnp.int32), start.astype(jnp.int32),
            end.astype(jnp.int32))


def _experts(items, xs, w1, w3, w2):
    bm = MOE_BM
    tile_of, exp_of, valid, start, end = items
    n_items = tile_of.shape[0]
    rows = pl.BlockSpec((bm * ROW_TILE, LANES), lambda i, t, e, *_: (t[i], 0))
    wspec = lambda a: pl.BlockSpec((None, None) + a.shape[2:], lambda i, t, e, *_: (0, e[i], 0, 0))
    return pl.pallas_call(
        _moe_kernel,
        grid_spec=pltpu.PrefetchScalarGridSpec(
            num_scalar_prefetch=5,
            grid=(n_items,),
            in_specs=[rows, wspec(w1), wspec(w3), wspec(w2)],
            out_specs=rows,
            scratch_shapes=[pltpu.VMEM(w1.shape[2:], bf16),
                            pltpu.VMEM(w3.shape[2:], bf16), pltpu.VMEM(w2.shape[2:], bf16)],
        ),
        out_shape=jax.ShapeDtypeStruct(xs.shape, f32),
        compiler_params=pltpu.CompilerParams(dimension_semantics=("arbitrary",), vmem_limit_bytes=VMEM_LIMIT),
        name="experts",
    )(tile_of, exp_of, valid, start, end, xs, w1, w3, w2)


def _combine_kernel(dest_ref, gw_ref, dest_next_ref, y_hbm, base_ref, g2_ref, b2_ref, o_ref,
                    buf_ref, acc_ref, sem):
    tt = COMBINE_TT
    i = pl.program_id(0)
    cur = i % 2

    def gather(d_ref, slot):
        def issue(j, carry):
            for k in range(TOP_K):
                pltpu.make_async_copy(_row_tile(y_hbm, d_ref[k, j]), _row_tile(buf_ref.at[slot, k], j),
                                      sem.at[slot]).start(priority=k % 2)
            return carry

        lax.fori_loop(0, tt, issue, 0, unroll=ROW_LOOP_UNROLL)

    @pl.when(i == 0)
    def _():
        gather(dest_ref, 0)

    @pl.when(i + 1 < pl.num_programs(0))
    def _():
        gather(dest_next_ref, 1 - cur)

    for k in range(TOP_K):
        pltpu.make_async_copy(y_hbm.at[pl.ds(0, tt * ROW_TILE), :], buf_ref.at[cur, k], sem.at[cur]).wait()

    def weigh(j, carry):
        rows = pl.ds(pl.multiple_of(j * ROW_TILE, ROW_TILE), ROW_TILE)
        tot = gw_ref[0, j] * buf_ref[cur, 0, rows, :]
        for k in range(1, TOP_K):
            tot = tot + gw_ref[k, j] * buf_ref[cur, k, rows, :]
        acc_ref[rows, :] = tot
        return carry

    lax.fori_loop(0, tt, weigh, 0, unroll=ROW_LOOP_UNROLL)
    routed = jnp.concatenate([acc_ref[pl.ds(s, tt, stride=ROW_TILE), :] for s in range(ROW_TILE)], axis=1)
    o_ref[...] = _layernorm(base_ref[...] + routed, g2_ref[...], b2_ref[...])


def _combine(dest_t, gw_t, y_rows, base, ln2_g, ln2_b):
    T = base.shape[0]
    tt = COMBINE_TT
    n_steps = T // tt
    slot = pl.BlockSpec((TOP_K, tt), lambda i: (0, i), memory_space=pltpu.SMEM)
    slot_next = pl.BlockSpec((TOP_K, tt), lambda i: (0, jnp.minimum(i + 1, n_steps - 1)), memory_space=pltpu.SMEM)
    vec = pl.BlockSpec((1, D_MODEL), lambda i: (0, 0))
    rows = pl.BlockSpec((tt, D_MODEL), lambda i: (i, 0))
    return pl.pallas_call(
        _combine_kernel,
        grid=(n_steps,),
        in_specs=[slot, slot, slot_next, pl.BlockSpec(memory_space=pl.ANY), rows, vec, vec],
        out_specs=rows,
        scratch_shapes=[pltpu.VMEM((2, TOP_K, tt * ROW_TILE, LANES), f32), pltpu.VMEM((tt * ROW_TILE, LANES), f32),
                        pltpu.SemaphoreType.DMA((2,))],
        out_shape=jax.ShapeDtypeStruct((T, D_MODEL), f32),
        compiler_params=pltpu.CompilerParams(dimension_semantics=("arbitrary",), vmem_limit_bytes=VMEM_LIMIT),
        name="combine",
    )(dest_t, gw_t, dest_t, y_rows, base, ln2_g.reshape(1, -1), ln2_b.reshape(1, -1))


def _layer(x, w_in, g_q, w_uq, g_kv, w_ukv, g_out_dil, g_out_mla, w_o, ln1_g, ln1_b,
           router_w, router_bias, w1, w3, w2, ws1, ws3, ws2, ln2_g, ln2_b, layer):
    batch, seq, _ = x.shape
    T = batch * seq
    x2d = x.reshape(T, D_MODEL)
    qd, kd, vd, qm, km, vm = _project(x2d, w_in[layer], w_uq[layer], w_ukv[layer], g_q[layer], g_kv[layer], seq)
    o_dil = _dilated_attention(qd, kd, vd, batch, seq)
    o_mla = _mla_attention(qm, km, vm, batch, seq)
    base, h_rows, e_t, rank_t, gw_t, counts = _post_attention(
        x2d, o_dil, o_mla, g_out_dil[layer], g_out_mla[layer], w_o[layer], ln1_g[layer], ln1_b[layer],
        router_w[layer], ws1[layer], ws3[layer], ws2[layer], router_bias[layer])
    items = _work_items(counts.reshape(N_EXPERTS), T * TOP_K)
    dest_t = _destinations(e_t, rank_t, items[3])
    xs = _dispatch(dest_t, h_rows)
    y_rows = _experts(items, xs, w1[layer:layer + 1], w3[layer:layer + 1], w2[layer:layer + 1])
    out = _combine(dest_t, gw_t, y_rows, base, ln2_g[layer], ln2_b[layer])
    return out.reshape(batch, seq, D_MODEL)


def kernel(x, w_in, g_q, w_uq, g_kv, w_ukv, g_out_dil, g_out_mla, w_o, ln1_g, ln1_b, router_w, router_bias,
           w1, w3, w2, ws1, ws3, ws2, ln2_g, ln2_b):
    assert x.shape[1] % DIL_TQ == 0 and x.shape[2] == D_MODEL
    h = x
    for layer in range(w_in.shape[0]):
        h = _layer(h, w_in, g_q, w_uq, g_kv, w_ukv, g_out_dil, g_out_mla, w_o, ln1_g, ln1_b,
                   router_w, router_bias, w1, w3, w2, ws1, ws3, ws2, ln2_g, ln2_b, layer)
    return h
```
